```python
import math
import jax, jax.numpy as jnp
from jax import lax
import numpy as np

D_MODEL = 1024
BATCH = 8
SEQ = 4096
DEPTH = 4

CHUNK = 64
N_BRANCH = 4
BRANCH_WIDTH = D_MODEL // 2
CONV_WIDTH = 4
NORM_EPS = 1e-6

RW_HEAD = 64
RW_HEADS = BRANCH_WIDTH // RW_HEAD
RW_LORA = 64
RW_GN_EPS = 64e-5

RET_HEADS = 8
RET_V_HEAD = BRANCH_WIDTH // RET_HEADS
RET_QK_HEAD = RET_V_HEAD // 2
RET_QK_WIDTH = RET_HEADS * RET_QK_HEAD
ROPE_BASE = 10000.0

SSD_HEAD = 64
SSD_HEADS = BRANCH_WIDTH // SSD_HEAD
SSD_GROUPS = 2
SSD_STATE = 128
SSD_CONV_DIM = BRANCH_WIDTH + 2 * SSD_GROUPS * SSD_STATE

GDN_HEAD = 128
GDN_HEADS = BRANCH_WIDTH // GDN_HEAD
GDN_CONV_DIM = 3 * BRANCH_WIDTH

IN_WIDTHS = (
    BRANCH_WIDTH, BRANCH_WIDTH, BRANCH_WIDTH, RW_LORA, RW_LORA, BRANCH_WIDTH,
    RET_QK_WIDTH, RET_QK_WIDTH, BRANCH_WIDTH, BRANCH_WIDTH,
    SSD_CONV_DIM, BRANCH_WIDTH, SSD_HEADS,
    GDN_CONV_DIM, BRANCH_WIDTH, GDN_HEADS, GDN_HEADS,
    N_BRANCH * D_MODEL,
)
N_IN = sum(IN_WIDTHS)

kernel_name = 'hybrid_rwkv7_retnet_ssd_gdn_stream_encoder'


def _split(p, widths):
    idx = np.cumsum(widths)[:-1].tolist()
    return jnp.split(p, idx, axis=-1)


def _rms(x, eps=NORM_EPS):
    xf = x.astype(jnp.float32)
    return (xf * lax.rsqrt(jnp.mean(xf * xf, axis=-1, keepdims=True) + eps)).astype(x.dtype)


def _rmsnorm(x, w):
    return _rms(x) * w


def _l2norm(x, eps=1e-6):
    xf = x.astype(jnp.float32)
    return (xf * lax.rsqrt(jnp.sum(xf * xf, axis=-1, keepdims=True) + eps)).astype(x.dtype)


def _head_group_norm(y, w, b, eps):
    yf = y.astype(jnp.float32)
    mu = jnp.mean(yf, axis=-1, keepdims=True)
    var = jnp.mean(jnp.square(yf - mu), axis=-1, keepdims=True)
    yn = (yf - mu) * lax.rsqrt(var + eps)
    bsz, s, h, n = y.shape
    return yn.reshape(bsz, s, h * n).astype(y.dtype) * w + b


def _shift_mix(t, mu):
    prev = jnp.pad(t, ((0, 0), (1, 0), (0, 0)))[:, :-1]
    return t + (prev - t) * mu


def _causal_conv(x, w):
    k, c = w.shape
    return lax.conv_general_dilated(
        x, w[:, None, :].astype(x.dtype), window_strides=(1,), padding=((k - 1, 0),),
        dimension_numbers=('NWC', 'WIO', 'NWC'), feature_group_count=c)


def _rotary(t):
    s = t.shape[1]
    half = t.shape[-1] // 2
    angle = 1.0 / (ROPE_BASE ** jnp.linspace(0.0, 1.0, half, dtype=jnp.float32))
    theta = jnp.arange(s, dtype=jnp.float32)[:, None] * angle[None, :]
    cos = jnp.cos(theta)[None, :, None, :]
    sin = jnp.sin(theta)[None, :, None, :]
    t2 = t.astype(jnp.float32).reshape(*t.shape[:-1], half, 2)
    x1, x2 = t2[..., 0], t2[..., 1]
    out = jnp.stack([x1 * cos - x2 * sin, x1 * sin + x2 * cos], axis=-1)
    return out.reshape(t.shape).astype(t.dtype)


def _chunked_decay_attention(q, k, v, log_a):
    b, s, h, dk = q.shape
    dv = v.shape[-1]
    n = s // CHUNK
    qc = q.reshape(b, n, CHUNK, h, dk)
    kc = k.reshape(b, n, CHUNK, h, dk)
    vc = v.reshape(b, n, CHUNK, h, dv)
    g = jnp.cumsum(log_a.astype(jnp.float32).reshape(b, n, CHUNK, h), axis=2)
    g_last = g[:, :, -1:, :]
    causal = jnp.tril(jnp.ones((CHUNK, CHUNK), dtype=bool))
    gt = jnp.swapaxes(g, 2, 3)
    decay = jnp.exp(jnp.where(causal, gt[..., :, None] - gt[..., None, :], -jnp.inf))
    scores = jnp.einsum('bnihk,bnjhk->bnhij', qc, kc) * decay
    y_intra = jnp.einsum('bnhij,bnjhv->bnihv', scores, vc)
    k_tail = kc * jnp.exp(g_last - g)[..., None]
    chunk_state = jnp.einsum('bnjhk,bnjhv->nbhkv', k_tail, vc)
    chunk_decay = jnp.transpose(jnp.exp(g_last[:, :, 0, :]), (1, 0, 2))

    def step(state, inp):
        st, dc = inp
        return state * dc[..., None, None] + st, state

    init = jnp.zeros((b, h, dk, dv), jnp.float32)
    _, prev = lax.scan(step, init, (chunk_state, chunk_decay))
    y_inter = jnp.einsum('bnihk,nbhkv->bnihv', qc * jnp.exp(g)[..., None], prev)
    return (y_intra + y_inter).reshape(b, s, h, dv).astype(v.dtype)


def _chunked_gated_delta(q, k, v, log_a, beta):
    b, s, h, dk = q.shape
    dv = v.shape[-1]
    n = s // CHUNK
    f32 = jnp.float32

    def to_chunks(t):
        return t.reshape(b, n, CHUNK, h, t.shape[-1]).transpose(1, 0, 3, 2, 4).astype(f32)

    qc, kc, vc = to_chunks(q), to_chunks(k), to_chunks(v)
    bc = to_chunks(beta[..., None])
    g = jnp.cumsum(to_chunks(log_a[..., None])[..., 0], axis=-1)
    eye = jnp.eye(CHUNK, dtype=f32)
    incl = jnp.tril(jnp.ones((CHUNK, CHUNK), dtype=bool))
    strict = jnp.tril(jnp.ones((CHUNK, CHUNK), dtype=bool), -1)
    decay = jnp.exp(jnp.where(incl, g[..., :, None] - g[..., None, :], -jnp.inf))
    kb = kc * bc
    m = jnp.where(strict, jnp.einsum('nbhik,nbhjk->nbhij', kb, kc) * decay, 0.0)
    t_inv = lax.linalg.triangular_solve(eye + m, jnp.broadcast_to(eye, m.shape),
                                        left_side=True, lower=True, unit_diagonal=True)
    u = t_inv @ (vc * bc)
    w = t_inv @ (kb * jnp.exp(g)[..., None])
    attn = jnp.einsum('nbhik,nbhjk->nbhij', qc, kc) * decay
    q_dec = qc * jnp.exp(g)[..., None]
    k_tail = kc * jnp.exp(g[..., -1:] - g)[..., None]
    chunk_decay = jnp.exp(g[..., -1])

    def step(state, inp):
        u_i, w_i, attn_i, qd_i, kt_i, dc_i = inp
        v_new = u_i - w_i @ state
        o = qd_i @ state + attn_i @ v_new
        state = state * dc_i[..., None, None] + jnp.swapaxes(kt_i, -1, -2) @ v_new
        return state, o

    init = jnp.zeros((b, h, dk, dv), f32)
    _, o = lax.scan(step, init, (u, w, attn, q_dec, k_tail, chunk_decay))
    return o.transpose(1, 0, 3, 2, 4).reshape(b, s, h, dv).astype(v.dtype)


def _rwkv7_scan(r, w, k, v, a, bvec):
    bsz, s, h, n = r.shape

    def step(state, inp):
        r_t, w_t, k_t, v_t, a_t, b_t = inp
        sa = jnp.einsum('bhvk,bhk->bhv', state, a_t)
        state = (state * w_t[:, :, None, :] + sa[..., None] * b_t[:, :, None, :]
                 + v_t[..., None] * k_t[:, :, None, :])
        return state, jnp.einsum('bhvk,bhk->bhv', state, r_t)

    xs = tuple(jnp.moveaxis(t, 1, 0) for t in (r, w, k, v, a, bvec))
    init = jnp.zeros((bsz, h, n, n), jnp.float32)
    _, y = lax.scan(step, init, xs)
    return jnp.moveaxis(y, 0, 1)


def _rwkv7_branch(r, k, v, w_lo, a_lo, z, mu_rkv, mu_wa, w_up, w0, a_up, a0, k_k, k_a, r_k, ln_w, ln_b):
    b, s, _ = r.shape
    heads = lambda t: t.reshape(b, s, RW_HEADS, RW_HEAD)
    r = _shift_mix(r, mu_rkv[0])
    k = _shift_mix(k, mu_rkv[1])
    v = _shift_mix(v, mu_rkv[2])
    w_lo = _shift_mix(w_lo, mu_wa[0])
    a_lo = _shift_mix(a_lo, mu_wa[1])
    w_log = -jax.nn.softplus(-(w0 + jnp.tanh(w_lo) @ w_up).astype(jnp.float32)) - 0.5
    decay = jnp.exp(-jnp.exp(w_log))
    iclr = jax.nn.sigmoid(a0 + a_lo @ a_up)
    kk = _l2norm(heads(k * k_k))
    k = k * (1.0 + (iclr - 1.0) * k_a)
    rh, kh, vh = heads(r), heads(k), heads(v)
    y = _rwkv7_scan(rh, heads(decay), kh, vh, -kk, kk * heads(iclr))
    y = _head_group_norm(y, ln_w, ln_b, RW_GN_EPS)
    bonus = (jnp.sum(rh * kh * r_k, axis=-1, keepdims=True) * vh).reshape(b, s, BRANCH_WIDTH)
    return (y + bonus) * jax.nn.silu(z)


def _retention_branch(q, k, v, z, norm_w):
    b, s, _ = q.shape
    qh = _rotary(q.reshape(b, s, RET_HEADS, RET_QK_HEAD))
    kh = _rotary(k.reshape(b, s, RET_HEADS, RET_QK_HEAD)) * (RET_QK_HEAD ** -0.5)
    vh = v.reshape(b, s, RET_HEADS, RET_V_HEAD)
    log_gamma = jnp.log(1.0 - jnp.exp2(-5.0 - jnp.arange(RET_HEADS, dtype=jnp.float32)))
    log_a = jnp.broadcast_to(log_gamma, (b, s, RET_HEADS))
    y = _chunked_decay_attention(qh, kh, vh, log_a)
    y = _rms(y).reshape(b, s, BRANCH_WIDTH) * norm_w
    return y * jax.nn.silu(z)


def _ssd_branch(xbc, z, dt, conv_w, conv_b, dt_bias, a_log, d_skip, norm_w):
    b, s, _ = xbc.shape
    xbc = jax.nn.silu(_causal_conv(xbc, conv_w) + conv_b)
    xs, bm, cm = _split(xbc, (BRANCH_WIDTH, SSD_GROUPS * SSD_STATE, SSD_GROUPS * SSD_STATE))
    rep = SSD_HEADS // SSD_GROUPS
    xh = xs.reshape(b, s, SSD_HEADS, SSD_HEAD)
    bh = jnp.repeat(bm.reshape(b, s, SSD_GROUPS, SSD_STATE), rep, axis=2)
    ch = jnp.repeat(cm.reshape(b, s, SSD_GROUPS, SSD_STATE), rep, axis=2)
    dt = jax.nn.softplus(dt.astype(jnp.float32) + dt_bias)
    a = -jnp.exp(a_log.astype(jnp.float32))
    y = _chunked_decay_attention(ch, bh, xh * dt[..., None], dt * a)
    y = y + xh * d_skip[:, None]
    y = y.reshape(b, s, BRANCH_WIDTH) * jax.nn.silu(z)
    y = _rms(y.reshape(b, s, SSD_GROUPS, BRANCH_WIDTH // SSD_GROUPS)).reshape(b, s, BRANCH_WIDTH)
    return y * norm_w


def _gdn_branch(qkv, z, beta_raw, alpha_raw, conv_w, dt_bias, a_log, norm_w):
    b, s, _ = qkv.shape
    qkv = jax.nn.silu(_causal_conv(qkv, conv_w))
    q, k, v = _split(qkv, (BRANCH_WIDTH, BRANCH_WIDTH, BRANCH_WIDTH))
    heads = lambda t: t.reshape(b, s, GDN_HEADS, GDN_HEAD)
    q = _l2norm(heads(q)) * (GDN_HEAD ** -0.5)
    k = _l2norm(heads(k))
    beta = jax.nn.sigmoid(beta_raw)
    log_a = -jnp.exp(a_log.astype(jnp.float32)) * jax.nn.softplus(alpha_raw.astype(jnp.float32) + dt_bias)
    y = _chunked_gated_delta(q, k, heads(v), log_a, beta)
    y = (_rms(y) * norm_w).reshape(b, s, BRANCH_WIDTH)
    return y * jax.nn.silu(z)


def _layer(x, norm_w, w_in, mu_rkv, mu_wa, w_up, w0, a_up, a0, k_k, k_a, r_k, ln_w, ln_b,
           ret_norm_w, ssd_conv_w, ssd_conv_b, ssd_dt_bias, ssd_a_log, ssd_d, ssd_norm_w,
           gdn_conv_w, gdn_dt_bias, gdn_a_log, gdn_norm_w, w_branch, w_out):
    b, s, _ = x.shape
    h = _rmsnorm(x, norm_w)
    p = h @ w_in
    (rw_r, rw_k, rw_v, rw_wlo, rw_alo, rw_z,
     rt_q, rt_k, rt_v, rt_z,
     sd_xbc, sd_z, sd_dt,
     gd_qkv, gd_z, gd_b, gd_a,
     gate_logits) = _split(p, IN_WIDTHS)
    u_a = _rwkv7_branch(rw_r, rw_k, rw_v, rw_wlo, rw_alo, rw_z, mu_rkv, mu_wa,
                        w_up, w0, a_up, a0, k_k, k_a, r_k, ln_w, ln_b)
    u_b = _retention_branch(rt_q, rt_k, rt_v, rt_z, ret_norm_w)
    u_c = _ssd_branch(sd_xbc, sd_z, sd_dt, ssd_conv_w, ssd_conv_b, ssd_dt_bias, ssd_a_log, ssd_d, ssd_norm_w)
    u_d = _gdn_branch(gd_qkv, gd_z, gd_b, gd_a, gdn_conv_w, gdn_dt_bias, gdn_a_log, gdn_norm_w)
    gates = jax.nn.sigmoid(gate_logits.reshape(b, s, N_BRANCH, D_MODEL))
    merged = gates[:, :, 0] * (u_a @ w_branch[0])
    merged = merged + gates[:, :, 1] * (u_b @ w_branch[1])
    merged = merged + gates[:, :, 2] * (u_c @ w_branch[2])
    merged = merged + gates[:, :, 3] * (u_d @ w_branch[3])
    return x + merged @ w_out


def setup_inputs(seed: int = 0) -> dict:
    key = jax.random.key(seed)
    ks = iter(jax.random.split(key, 32))
    f32 = jnp.float32
    L, W, D = DEPTH, BRANCH_WIDTH, D_MODEL

    def normal(shape, scale):
        return jax.random.normal(next(ks), shape, f32) * scale

    def uniform(shape, lo, hi):
        return jax.random.uniform(next(ks), shape, f32, lo, hi)

    def dt_bias(shape):
        dt = jnp.exp(uniform(shape, math.log(1e-3), math.log(1e-1)))
        return dt + jnp.log(-jnp.expm1(-dt))

    return {
        'x': normal((BATCH, SEQ, D), 1.0),
        'norm_w': 1.0 + normal((L, D), 0.02),
        'w_in': normal((L, D, N_IN), D ** -0.5),
        'rwkv_mu_rkv': uniform((L, 3, W), 0.0, 1.0),
        'rwkv_mu_wa': uniform((L, 2, RW_LORA), 0.0, 1.0),
        'rwkv_w_up': normal((L, RW_LORA, W), 0.5 * RW_LORA ** -0.5),
        'rwkv_w0': uniform((L, W), -6.0, 1.0),
        'rwkv_a_up': normal((L, RW_LORA, W), 0.5 * RW_LORA ** -0.5),
        'rwkv_a0': normal((L, W), 0.1),
        'rwkv_k_k': 0.85 + normal((L, W), 0.02),
        'rwkv_k_a': 1.0 + normal((L, W), 0.02),
        'rwkv_r_k': normal((L, RW_HEADS, RW_HEAD), 0.1),
        'rwkv_ln_w': 1.0 + normal((L, W), 0.02),
        'rwkv_ln_b': normal((L, W), 0.02),
        'ret_norm_w': 1.0 + normal((L, W), 0.02),
        'ssd_conv_w': normal((L, CONV_WIDTH, SSD_CONV_DIM), CONV_WIDTH ** -0.5),
        'ssd_conv_b': normal((L, SSD_CONV_DIM), 0.02),
        'ssd_dt_bias': dt_bias((L, SSD_HEADS)),
        'ssd_A_log': jnp.log(uniform((L, SSD_HEADS), 1.0, 16.0)),
        'ssd_D': 1.0 + normal((L, SSD_HEADS), 0.02),
        'ssd_norm_w': 1.0 + normal((L, W), 0.02),
        'gdn_conv_w': normal((L, CONV_WIDTH, GDN_CONV_DIM), CONV_WIDTH ** -0.5),
        'gdn_dt_bias': dt_bias((L, GDN_HEADS)),
        'gdn_A_log': jnp.log(uniform((L, GDN_HEADS), 1.0, 16.0)),
        'gdn_norm_w': 1.0 + normal((L, GDN_HEAD), 0.02),
        'w_branch': normal((L, N_BRANCH, W, D), W ** -0.5),
        'w_out': normal((L, D, D), D ** -0.5),
        'final_norm_w': 1.0 + normal((D,), 0.02),
    }


def reference(x, norm_w, w_in, rwkv_mu_rkv, rwkv_mu_wa, rwkv_w_up, rwkv_w0, rwkv_a_up, rwkv_a0,
              rwkv_k_k, rwkv_k_a, rwkv_r_k, rwkv_ln_w, rwkv_ln_b, ret_norm_w, ssd_conv_w, ssd_conv_b,
              ssd_dt_bias, ssd_A_log, ssd_D, ssd_norm_w, gdn_conv_w, gdn_dt_bias, gdn_A_log, gdn_norm_w,
              w_branch, w_out, final_norm_w):
    for i in range(DEPTH):
        x = _layer(x, norm_w[i], w_in[i], rwkv_mu_rkv[i], rwkv_mu_wa[i], rwkv_w_up[i], rwkv_w0[i],
                   rwkv_a_up[i], rwkv_a0[i], rwkv_k_k[i], rwkv_k_a[i], rwkv_r_k[i], rwkv_ln_w[i],
                   rwkv_ln_b[i], ret_norm_w[i], ssd_conv_w[i], ssd_conv_b[i], ssd_dt_bias[i],
                   ssd_A_log[i], ssd_D[i], ssd_norm_w[i], gdn_conv_w[i], gdn_dt_bias[i],
                   gdn_A_log[i], gdn_norm_w[i], w_branch[i], w_out[i])
    return _rmsnorm(x, final_norm_w)
```

```python
import functools
import math

import numpy as np
import jax
import jax.numpy as jnp
from jax import lax
from jax.experimental import pallas as pl
from jax.experimental.pallas import tpu as pltpu

F32 = jnp.float32
BF16 = jnp.bfloat16
HIGHEST = lax.Precision.HIGHEST

D_MODEL = 1024
WIDTH = 512
NORM_EPS = 1e-6
CONV_K = 4
RW_HEAD = 64
RW_LORA = 64
RW_GN_EPS = 64e-5
RET_HEADS = 8
RET_QK = 256
ROPE_BASE = 10000.0
SSD_HEADS = 8
SSD_STATE = 128
GDN_HEADS = 4
GDN_HEAD = 128

LANE = 128
SUBLANE = 8
CHUNK = 64
SSD_CHUNK = 128
TILE = 256
PROJ_TILE = 512
VMEM_LIMIT = 56 * 1024 * 1024


def _dot(a, b):
    return jnp.dot(a.astype(BF16), b.astype(BF16), preferred_element_type=F32)


def _dot_nt(a, b):
    return lax.dot_general(a.astype(BF16), b.astype(BF16), (((1,), (1,)), ((), ())),
                           preferred_element_type=F32)


def _dot_tn(a, b):
    return lax.dot_general(a.astype(BF16), b.astype(BF16), (((0,), (0,)), ((), ())),
                           preferred_element_type=F32)


def _dot_f32(a, b):
    return jnp.dot(a, b, preferred_element_type=F32, precision=HIGHEST)


def _sigmoid(x):
    return 1.0 / (1.0 + jnp.exp(-x))


def _silu(x):
    return x * _sigmoid(x)


def _softplus(x):
    return jnp.maximum(x, 0.0) + jnp.log(1.0 + jnp.exp(-jnp.abs(x)))


def _iota(shape, dim):
    return lax.broadcasted_iota(jnp.int32, shape, dim)


def _lane_mask(width, lo, hi):
    lane = _iota((1, width), 1)
    return ((lane >= lo) & (lane < hi)).astype(F32)


def _tri_chunk(n, chunk):
    r = _iota((n, n), 0)
    c = _iota((n, n), 1)
    sh = int(math.log2(chunk))
    return ((r >= c) & (jnp.right_shift(r, sh) == jnp.right_shift(c, sh))).astype(F32)


def _seg64_sum(x):
    outs = []
    for j in range(x.shape[1] // LANE):
        xb = x[:, LANE * j:LANE * (j + 1)]
        lo = _iota(xb.shape, 1) < 64
        s_lo = jnp.sum(jnp.where(lo, xb, 0.0), axis=-1, keepdims=True)
        s_hi = jnp.sum(jnp.where(lo, 0.0, xb), axis=-1, keepdims=True)
        outs.append(jnp.where(lo, s_lo, s_hi))
    return outs[0] if len(outs) == 1 else jnp.concatenate(outs, axis=-1)


def _shifted(hist_ref, first, cur, n_shift):
    tile = cur.shape[0]

    @pl.when(first)
    def _():
        hist_ref[0:SUBLANE, :] = jnp.zeros((SUBLANE, cur.shape[1]), F32)

    hist_ref[SUBLANE:SUBLANE + tile, :] = cur
    outs = [hist_ref[SUBLANE - j:SUBLANE - j + tile, :] for j in range(1, n_shift + 1)]
    hist_ref[0:SUBLANE, :] = cur[tile - SUBLANE:tile, :]
    return outs


def _causal_conv(hist_ref, first, cur, w_ref):
    d1, d2, d3 = _shifted(hist_ref, first, cur, CONV_K - 1)
    return (cur * w_ref[3:4, :] + d1 * w_ref[2:3, :] + d2 * w_ref[1:2, :] + d3 * w_ref[0:1, :])


def _neumann_inverse(n):
    size = n.shape[0]
    eye = (_iota((size, size), 0) == _iota((size, size), 1)).astype(F32)
    p = eye + n
    n = _dot_f32(n, n)
    levels = int(math.log2(CHUNK)) - 1
    for s in range(levels - 1):
        both = _dot_f32(jnp.concatenate([p, n], axis=0), n)
        p = p + both[:size]
        n = both[size:]
    return p + _dot_f32(p, n)


def _proj_kernel(x_ref, nw_ref, w_ref, o_ref):
    x = x_ref[...]
    h = x * lax.rsqrt(jnp.mean(x * x, axis=-1, keepdims=True) + NORM_EPS) * nw_ref[...]
    o_ref[...] = jnp.dot(h.astype(BF16), w_ref[...], preferred_element_type=F32)


def _project(x2, norm_w, w):
    tokens, d = x2.shape
    n = w.shape[1]
    return pl.pallas_call(
        _proj_kernel,
        grid=(tokens // PROJ_TILE,),
        in_specs=[pl.BlockSpec((PROJ_TILE, d), lambda i: (i, 0)),
                  pl.BlockSpec((1, d), lambda i: (0, 0)),
                  pl.BlockSpec((d, n), lambda i: (0, 0))],
        out_specs=pl.BlockSpec((PROJ_TILE, n), lambda i: (i, 0)),
        out_shape=jax.ShapeDtypeStruct((tokens, n), F32),
        name="in_proj",
        compiler_params=pltpu.CompilerParams(dimension_semantics=("arbitrary",),
                                             vmem_limit_bytes=VMEM_LIMIT),
    )(x2, norm_w, w)


def _ret_kernel(p_ref, cos_ref, sin_ref, dmat_ref, qdec_ref, ktail_ref, gam_ref, nw_ref,
                o_ref, state_ref):
    first = pl.program_id(1) == 0

    @pl.when(first)
    def _():
        state_ref[...] = jnp.zeros(state_ref.shape, F32)

    cos = cos_ref[...]
    sin = sin_ref[...]
    q = p_ref[:, 0:256] * cos + p_ref[:, 512:768] * sin
    k = (p_ref[:, 256:512] * cos + p_ref[:, 768:1024] * sin) * (32.0 ** -0.5)
    v = p_ref[:, 1024:1536]
    z = p_ref[:, 1536:2048]
    tile = q.shape[0]
    qmasks = [_lane_mask(LANE, 32 * h, 32 * h + 32) for h in range(4)]
    vmasks = [_lane_mask(256, 64 * h, 64 * h + 64) for h in range(4)]
    y_groups = []
    for g in range(2):
        qg = q[:, LANE * g:LANE * (g + 1)]
        kg = k[:, LANE * g:LANE * (g + 1)]
        vg = v[:, 256 * g:256 * (g + 1)]
        state = state_ref[g]
        dmat = dmat_ref[g]
        qdec = qdec_ref[g]
        ktail = ktail_ref[g]
        gam = gam_ref[g]
        ys = []
        for c in range(tile // CHUNK):
            rows = slice(c * CHUNK, (c + 1) * CHUNK)
            q_st = jnp.concatenate([qg[rows] * m for m in qmasks], axis=0)
            k_st = jnp.concatenate([kg[rows] * m for m in qmasks], axis=0)
            v_st = jnp.concatenate([vg[rows] * m for m in vmasks], axis=0)
            s = _dot_nt(q_st, k_st) * dmat
            y_st = _dot(s, v_st) + _dot(q_st * qdec, state)
            state = gam * state + _dot_tn(k_st * ktail, v_st)
            ys.append(y_st[0:CHUNK] + y_st[CHUNK:2 * CHUNK]
                      + y_st[2 * CHUNK:3 * CHUNK] + y_st[3 * CHUNK:4 * CHUNK])
        state_ref[g] = state
        y_groups.append(jnp.concatenate(ys, axis=0))
    y = jnp.concatenate(y_groups, axis=-1)
    ms = _seg64_sum(y * y) * (1.0 / 64.0)
    y = y * lax.rsqrt(ms + NORM_EPS) * nw_ref[...]
    o_ref[...] = (y * _silu(z)).astype(o_ref.dtype)


def _ret_tables(seq):
    half = 16
    angle = 1.0 / (ROPE_BASE ** jnp.linspace(0.0, 1.0, half, dtype=F32))
    theta = jnp.arange(seq, dtype=F32)[:, None] * angle[None, :]
    cos = jnp.tile(jnp.repeat(jnp.cos(theta), 2, axis=1), (1, RET_HEADS))
    sin = jnp.tile(jnp.repeat(jnp.sin(theta), 2, axis=1), (1, RET_HEADS))
    log_gamma = jnp.log(1.0 - jnp.exp2(-5.0 - jnp.arange(RET_HEADS, dtype=F32)))
    lg = log_gamma.reshape(2, 4)
    i = jnp.arange(CHUNK, dtype=F32)
    diff = i[:, None] - i[None, :]
    low = diff >= 0
    blocks = jnp.exp(jnp.where(low[None, None], lg[:, :, None, None] * diff[None, None], -jnp.inf))
    eye4 = jnp.eye(4, dtype=F32)
    dmat = jnp.einsum('ghij,hk->ghikj', blocks, eye4).reshape(2, 4 * CHUNK, 4 * CHUNK)
    qdec = jnp.exp(lg[:, :, None] * (i[None, None, :] + 1.0)).reshape(2, 4 * CHUNK, 1)
    qdec = jnp.broadcast_to(qdec, (2, 4 * CHUNK, LANE))
    ktail = jnp.exp(lg[:, :, None] * (CHUNK - 1.0 - i[None, None, :])).reshape(2, 4 * CHUNK, 1)
    ktail = jnp.broadcast_to(ktail, (2, 4 * CHUNK, LANE))
    gam_rows = jnp.repeat(jnp.exp(lg * CHUNK), 32, axis=1)
    gam = jnp.broadcast_to(gam_rows[:, :, None], (2, LANE, 256))
    return cos, sin, dmat, qdec, ktail, gam


def _ret_mixer(p, tables, norm_w):
    b, s, n = p.shape
    cos, sin, dmat, qdec, ktail, gam = tables
    full = lambda *shape: pl.BlockSpec(shape, lambda bi, i: (0,) * len(shape))
    return pl.pallas_call(
        _ret_kernel,
        grid=(b, s // TILE),
        in_specs=[pl.BlockSpec((None, TILE, n), lambda bi, i: (bi, i, 0)),
                  pl.BlockSpec((TILE, 256), lambda bi, i: (i, 0)),
                  pl.BlockSpec((TILE, 256), lambda bi, i: (i, 0)),
                  full(2, 4 * CHUNK, 4 * CHUNK), full(2, 4 * CHUNK, LANE), full(2, 4 * CHUNK, LANE),
                  full(2, LANE, 256), full(1, WIDTH)],
        out_specs=pl.BlockSpec((None, TILE, WIDTH), lambda bi, i: (bi, i, 0)),
        out_shape=jax.ShapeDtypeStruct((b, s, WIDTH), BF16),
        scratch_shapes=[pltpu.VMEM((2, LANE, 256), F32)],
        name="ret_mixer",
        compiler_params=pltpu.CompilerParams(dimension_semantics=("arbitrary", "arbitrary"),
                                             vmem_limit_bytes=VMEM_LIMIT),
    )(p, cos, sin, dmat, qdec, ktail, gam, norm_w)


def _ssd_kernel(p_ref, cw_ref, cb_ref, dtb_ref, a_ref, dskip_ref, nw_ref, e64_ref, e128_ref,
                o_ref, hist_ref, state_ref):
    first = pl.program_id(1) == 0

    @pl.when(first)
    def _():
        state_ref[...] = jnp.zeros(state_ref.shape, F32)

    xbc = _silu(_causal_conv(hist_ref, first, p_ref[:, 0:1024], cw_ref) + cb_ref[...])
    z = p_ref[:, 1024:1536]
    dt = _softplus(p_ref[:, 1536:1664] + dtb_ref[...])
    log_a = dt * a_ref[...]
    tile = xbc.shape[0]
    xs = xbc[:, 0:512]
    g_cum = _dot_f32(_tri_chunk(tile, SSD_CHUNK), log_a)
    g_all = _dot_f32(g_cum, e128_ref[...])
    val = xs * _dot_f32(dt, e64_ref[...])
    ri = _iota((SSD_CHUNK, SSD_CHUNK), 0)
    ci = _iota((SSD_CHUNK, SSD_CHUNK), 1)
    low = ri >= ci
    hmask = [_lane_mask(256, 64 * h, 64 * h + 64) for h in range(4)]
    y_groups = []
    for g in range(2):
        bm = xbc[:, 512 + LANE * g:512 + LANE * (g + 1)]
        cm = xbc[:, 768 + LANE * g:768 + LANE * (g + 1)]
        vg = val[:, 256 * g:256 * (g + 1)]
        ys = []
        for c in range(tile // SSD_CHUNK):
            rows = slice(c * SSD_CHUNK, (c + 1) * SSD_CHUNK)
            qk = _dot_nt(cm[rows], bm[rows])
            vc = vg[rows]
            s_cat, q_cat = [], []
            for h in range(4):
                head = 4 * g + h
                gc = g_all[rows, LANE * head:LANE * (head + 1)]
                dec = jnp.exp(jnp.where(low, gc - gc.T, -jnp.inf))
                s_cat.append(qk * dec)
                q_cat.append(cm[rows] * jnp.exp(gc))
            v_st = jnp.concatenate([vc * m for m in hmask], axis=0)
            y = _dot(jnp.concatenate(s_cat, axis=-1), v_st)
            y = y + _dot(jnp.concatenate(q_cat, axis=-1), state_ref[g])
            for h in range(4):
                head = 4 * g + h
                gc = g_all[rows, LANE * head:LANE * (head + 1)]
                g_last = gc[SSD_CHUNK - 1:SSD_CHUNK, :]
                k_tail = bm[rows] * jnp.exp(g_last - gc)
                upd = _dot_tn(k_tail, vc) * hmask[h]
                lam = jnp.concatenate([jnp.exp(g_last), jnp.exp(g_last)], axis=-1)
                blk = slice(SSD_STATE * h, SSD_STATE * (h + 1))
                state_ref[g, blk, :] = state_ref[g, blk, :] * lam + upd
            ys.append(y)
        y_groups.append(jnp.concatenate(ys, axis=0))
    y = jnp.concatenate(y_groups, axis=-1) + xs * dskip_ref[...]
    y = y * _silu(z)
    outs = []
    for g in range(2):
        yg = y[:, 256 * g:256 * (g + 1)]
        ms = jnp.sum(yg * yg, axis=-1, keepdims=True) * (1.0 / 256.0)
        outs.append(yg * lax.rsqrt(ms + NORM_EPS))
    o_ref[...] = (jnp.concatenate(outs, axis=-1) * nw_ref[...]).astype(o_ref.dtype)


def _ssd_mixer(p, conv_w, conv_b, dt_bias, a_vec, d_skip, norm_w, e64, e128):
    b, s, n = p.shape
    full = lambda *shape: pl.BlockSpec(shape, lambda bi, i: (0,) * len(shape))
    return pl.pallas_call(
        _ssd_kernel,
        grid=(b, s // TILE),
        in_specs=[pl.BlockSpec((None, TILE, n), lambda bi, i: (bi, i, 0)),
                  full(CONV_K, 1024), full(1, 1024), full(1, LANE), full(1, LANE),
                  full(1, WIDTH), full(1, WIDTH), full(LANE, WIDTH), full(LANE, SSD_HEADS * LANE)],
        out_specs=pl.BlockSpec((None, TILE, WIDTH), lambda bi, i: (bi, i, 0)),
        out_shape=jax.ShapeDtypeStruct((b, s, WIDTH), BF16),
        scratch_shapes=[pltpu.VMEM((SUBLANE + TILE, 1024), F32),
                        pltpu.VMEM((2, 4 * SSD_STATE, 256), F32)],
        name="ssd_mixer",
        compiler_params=pltpu.CompilerParams(dimension_semantics=("arbitrary", "arbitrary"),
                                             vmem_limit_bytes=VMEM_LIMIT),
    )(p, conv_w, conv_b, dt_bias, a_vec, d_skip, norm_w, e64, e128)


def _gdn_kernel(p_ref, cw_ref, dtb_ref, a_ref, nw_ref, eb_ref, eg_ref,
                o_ref, hist_ref, state_ref):
    first = pl.program_id(1) == 0

    @pl.when(first)
    def _():
        state_ref[...] = jnp.zeros(state_ref.shape, F32)

    qkv = _silu(_causal_conv(hist_ref, first, p_ref[:, 0:1536], cw_ref))
    z = p_ref[:, 1536:2048]
    ba = p_ref[:, 2048:2176]
    beta_blk = _sigmoid(ba)
    log_a = a_ref[...] * _softplus(ba + dtb_ref[...])
    tile = qkv.shape[0]
    g_all = _dot_f32(_dot_f32(_tri_chunk(tile, CHUNK), log_a), eg_ref[...])
    beta_all = _dot_f32(beta_blk, eb_ref[...])

    def l2n(t):
        return t * lax.rsqrt(jnp.sum(t * t, axis=-1, keepdims=True) + 1e-6)

    two = 2 * CHUNK
    ri = _iota((two, two), 0)
    ci = _iota((two, two), 1)
    same = jnp.right_shift(ri, 6) == jnp.right_shift(ci, 6)
    incl = same & (ri >= ci)
    strict = same & (ri > ci)
    zero = jnp.zeros((CHUNK, GDN_HEAD), F32)
    o_pairs = []
    for pr in range(2):
        heads = (2 * pr, 2 * pr + 1)
        qh = [l2n(qkv[:, LANE * h:LANE * (h + 1)]) * (GDN_HEAD ** -0.5) for h in heads]
        kh = [l2n(qkv[:, 512 + LANE * h:512 + LANE * (h + 1)]) for h in heads]
        vh = [qkv[:, 1024 + LANE * h:1024 + LANE * (h + 1)] for h in heads]
        state = state_ref[pr]
        os = []
        for c in range(tile // CHUNK):
            rows = slice(c * CHUNK, (c + 1) * CHUNK)

            def stack(t0, t1):
                return jnp.concatenate([jnp.concatenate([t0, zero], axis=-1),
                                        jnp.concatenate([zero, t1], axis=-1)], axis=0)

            gc = jnp.concatenate([g_all[rows, LANE * h:LANE * (h + 1)] for h in heads], axis=0)
            bt = jnp.concatenate([beta_all[rows, LANE * h:LANE * (h + 1)] for h in heads], axis=0)
            gc2 = jnp.concatenate([gc, gc], axis=-1)
            bt2 = jnp.concatenate([bt, bt], axis=-1)
            q_st = stack(qh[0][rows], qh[1][rows])
            k_st = stack(kh[0][rows], kh[1][rows])
            v_st = stack(vh[0][rows], vh[1][rows])
            kb_st = k_st * bt2
            diff = gc - gc.T
            dec_i = jnp.exp(jnp.where(incl, diff, -jnp.inf))
            sc = _dot_nt(jnp.concatenate([kb_st, q_st], axis=0), k_st)
            m = jnp.where(strict, sc[0:two] * dec_i, 0.0)
            attn = sc[two:2 * two] * dec_i
            t_inv = _neumann_inverse(-m)
            eg = jnp.exp(gc2)
            uw = _dot(t_inv, jnp.concatenate([v_st * bt2, kb_st * eg], axis=-1))
            u = uw[:, 0:256]
            w = uw[:, 256:512]
            v_new = u - _dot(w, state)
            o_st = _dot(q_st * eg, state) + _dot(attn, v_new)
            g_last = jnp.concatenate(
                [jnp.broadcast_to(gc2[CHUNK - 1:CHUNK, :], (CHUNK, 256)),
                 jnp.broadcast_to(gc2[two - 1:two, :], (CHUNK, 256))], axis=0)
            k_tail = k_st * jnp.exp(g_last - gc2)
            lam = jnp.concatenate(
                [jnp.broadcast_to(jnp.exp(gc2[CHUNK - 1:CHUNK, :]), (GDN_HEAD, 256)),
                 jnp.broadcast_to(jnp.exp(gc2[two - 1:two, :]), (GDN_HEAD, 256))], axis=0)
            state = state * lam + _dot_tn(k_tail, v_new)
            os.append(o_st[0:CHUNK] + o_st[CHUNK:two])
        state_ref[pr] = state
        o_pairs.append(jnp.concatenate(os, axis=0))
    outs = []
    for pr in range(2):
        for j in range(2):
            oh = o_pairs[pr][:, LANE * j:LANE * (j + 1)]
            ms = jnp.sum(oh * oh, axis=-1, keepdims=True) * (1.0 / GDN_HEAD)
            outs.append(oh * lax.rsqrt(ms + NORM_EPS) * nw_ref[...])
    o_ref[...] = (jnp.concatenate(outs, axis=-1) * _silu(z)).astype(o_ref.dtype)


def _gdn_mixer(p, conv_w, dt_bias, a_vec, norm_w, e_beta, e_g):
    b, s, n = p.shape
    full = lambda *shape: pl.BlockSpec(shape, lambda bi, i: (0,) * len(shape))
    return pl.pallas_call(
        _gdn_kernel,
        grid=(b, s // TILE),
        in_specs=[pl.BlockSpec((None, TILE, n), lambda bi, i: (bi, i, 0)),
                  full(CONV_K, 1536), full(1, LANE), full(1, LANE), full(1, GDN_HEAD),
                  full(LANE, GDN_HEADS * LANE), full(LANE, GDN_HEADS * LANE)],
        out_specs=pl.BlockSpec((None, TILE, WIDTH), lambda bi, i: (bi, i, 0)),
        out_shape=jax.ShapeDtypeStruct((b, s, WIDTH), BF16),
        scratch_shapes=[pltpu.VMEM((SUBLANE + TILE, 1536), F32),
                        pltpu.VMEM((2, 256, 256), F32)],
        name="gdn_mixer",
        compiler_params=pltpu.CompilerParams(dimension_semantics=("arbitrary", "arbitrary"),
                                             vmem_limit_bytes=VMEM_LIMIT),
    )(p, conv_w, dt_bias, a_vec, norm_w, e_beta, e_g)


def _rwkv_kernel(p_ref, mu_ref, wup_ref, w0_ref, aup_ref, a0_ref, kk_ref, ka_ref,
                 rk_ref, lnw_ref, lnb_ref, o_ref, hist_ref, state_ref):
    first = pl.program_id(1) == 0

    @pl.when(first)
    def _():
        state_ref[...] = jnp.zeros(state_ref.shape, F32)

    cur = p_ref[:, 0:1664]
    (prev,) = _shifted(hist_ref, first, cur, 1)
    mixed = cur + (prev - cur) * mu_ref[...]
    r = mixed[:, 0:512]
    k = mixed[:, 512:1024]
    v = mixed[:, 1024:1536]
    lo = mixed[:, 1536:1664]
    z = p_ref[:, 1664:2176]
    tile = r.shape[0]
    w_log = -_softplus(-(w0_ref[...] + _dot(jnp.tanh(lo), wup_ref[...]))) - 0.5
    logw = -jnp.exp(w_log)
    iclr = _sigmoid(a0_ref[...] + _dot(lo, aup_ref[...]))
    kkr = k * kk_ref[...]
    kk = kkr * lax.rsqrt(_seg64_sum(kkr * kkr) + 1e-6)
    k = k * (1.0 + (iclr - 1.0) * ka_ref[...])
    a = -kk
    b = kk * iclr
    g_inc = _dot_f32(_tri_chunk(tile, CHUNK), logw)
    g_exc = g_inc - logw
    e_inc = jnp.exp(g_inc)
    e_neg = jnp.exp(-g_inc)
    r_t = r * e_inc
    a_t = a * jnp.exp(g_exc)
    k_h = k * e_neg
    b_h = b * e_neg

    two = 2 * CHUNK
    ri = _iota((two, two), 0)
    ci = _iota((two, two), 1)
    same = jnp.right_shift(ri, 6) == jnp.right_shift(ci, 6)
    incl = (same & (ri >= ci)).astype(F32)
    strict = (same & (ri > ci)).astype(F32)
    m0 = _lane_mask(LANE, 0, 64)
    m1 = _lane_mask(LANE, 64, LANE)

    def stack(t):
        return jnp.concatenate([t * m0, t * m1], axis=0)

    y_pairs = []
    for pr in range(4):
        lanes = slice(LANE * pr, LANE * (pr + 1))
        state = state_ref[pr]
        ys = []
        for c in range(tile // CHUNK):
            rows = slice(c * CHUNK, (c + 1) * CHUNK)
            a_st = stack(a_t[rows, lanes])
            r_st = stack(r_t[rows, lanes])
            k_st = stack(k_h[rows, lanes])
            b_st = stack(b_h[rows, lanes])
            v_st = stack(v[rows, lanes])
            sc = _dot_nt(jnp.concatenate([a_st, r_st], axis=0),
                         jnp.concatenate([k_st, b_st], axis=0))
            a_ak = sc[0:two, 0:two] * strict
            a_ab = sc[0:two, two:2 * two] * strict
            a_rk = sc[two:2 * two, 0:two] * incl
            a_rb = sc[two:2 * two, two:2 * two] * incl
            t_inv = _neumann_inverse(a_ab)
            xs = _dot_nt(jnp.concatenate([a_st, r_st], axis=0), state)
            av = _dot(jnp.concatenate([a_ak, a_rk], axis=0), v_st)
            u = _dot(t_inv, xs[0:two] + av[0:two])
            y_st = xs[two:2 * two] + av[two:2 * two] + _dot(a_rb, u)
            g_last = g_inc[(c + 1) * CHUNK - 1:(c + 1) * CHUNK, lanes]
            tail = jnp.exp(g_last - g_inc[rows, lanes])
            kt_st = stack(k[rows, lanes] * tail)
            bt_st = stack(b[rows, lanes] * tail)
            state = state * jnp.exp(g_last) + _dot_tn(
                jnp.concatenate([v_st, u], axis=0), jnp.concatenate([kt_st, bt_st], axis=0))
            ys.append(y_st[0:CHUNK] + y_st[CHUNK:two])
        state_ref[pr] = state
        y_pairs.append(jnp.concatenate(ys, axis=0))
    y = jnp.concatenate(y_pairs, axis=-1)
    mu = _seg64_sum(y) * (1.0 / RW_HEAD)
    yc = y - mu
    var = _seg64_sum(yc * yc) * (1.0 / RW_HEAD)
    yn = yc * lax.rsqrt(var + RW_GN_EPS) * lnw_ref[...] + lnb_ref[...]
    bonus = _seg64_sum(r * k * rk_ref[...]) * v
    o_ref[...] = ((yn + bonus) * _silu(z)).astype(o_ref.dtype)


def _rwkv_mixer(p, mu, w_up, w0, a_up, a0, k_k, k_a, r_k, ln_w, ln_b):
    b, s, n = p.shape
    full = lambda *shape: pl.BlockSpec(shape, lambda bi, i: (0,) * len(shape))
    return pl.pallas_call(
        _rwkv_kernel,
        grid=(b, s // TILE),
        in_specs=[pl.BlockSpec((None, TILE, n), lambda bi, i: (bi, i, 0)),
                  full(1, 1664)] + [full(LANE, WIDTH), full(1, WIDTH)] * 2
                 + [full(1, WIDTH)] * 5,
        out_specs=pl.BlockSpec((None, TILE, WIDTH), lambda bi, i: (bi, i, 0)),
        out_shape=jax.ShapeDtypeStruct((b, s, WIDTH), BF16),
        scratch_shapes=[pltpu.VMEM((SUBLANE + TILE, 1664), F32),
                        pltpu.VMEM((4, LANE, LANE), F32)],
        name="rwkv_mixer",
        compiler_params=pltpu.CompilerParams(dimension_semantics=("arbitrary", "arbitrary"),
                                             vmem_limit_bytes=VMEM_LIMIT),
    )(p, mu, w_up, w0, a_up, a0, k_k, k_a, r_k, ln_w, ln_b)


def _merge_kernel(x_ref, nw_ref, wg_ref, ua_ref, ub_ref, uc_ref, ud_ref, wb_ref, wo_ref, fw_ref,
                  o_ref, *, final_norm):
    x = x_ref[...]
    h = (x * lax.rsqrt(jnp.mean(x * x, axis=-1, keepdims=True) + NORM_EPS) * nw_ref[...]).astype(BF16)
    merged = None
    for i, u_ref in enumerate((ua_ref, ub_ref, uc_ref, ud_ref)):
        gate = _sigmoid(jnp.dot(h, wg_ref[:, D_MODEL * i:D_MODEL * (i + 1)],
                                preferred_element_type=F32))
        term = gate * jnp.dot(u_ref[...], wb_ref[i], preferred_element_type=F32)
        merged = term if merged is None else merged + term
    out = x + jnp.dot(merged.astype(BF16), wo_ref[...], preferred_element_type=F32)
    if final_norm:
        out = out * lax.rsqrt(jnp.mean(out * out, axis=-1, keepdims=True) + NORM_EPS) * fw_ref[...]
    o_ref[...] = out


def _merge(x2, norm_w, w_gate, us, w_branch, w_out, final_w, final_norm):
    tokens, d = x2.shape
    row = lambda n: pl.BlockSpec((PROJ_TILE, n), lambda i: (i, 0))
    const = lambda *shape: pl.BlockSpec(shape, lambda i: (0,) * len(shape))
    return pl.pallas_call(
        functools.partial(_merge_kernel, final_norm=final_norm),
        grid=(tokens // PROJ_TILE,),
        in_specs=[row(d), const(1, d), const(d, 4 * d)] + [row(WIDTH)] * 4
                 + [const(4, WIDTH, d), const(d, d), const(1, d)],
        out_specs=row(d),
        out_shape=jax.ShapeDtypeStruct((tokens, d), F32),
        name="merge",
        compiler_params=pltpu.CompilerParams(dimension_semantics=("arbitrary",),
                                             vmem_limit_bytes=VMEM_LIMIT),
    )(x2, norm_w, w_gate, *us, w_branch, w_out, final_w)


def _pad_cols(w, n):
    return jnp.pad(w, ((0, 0), (0, n - w.shape[1])))


def _split_w_in(w_in):
    widths = (WIDTH, WIDTH, WIDTH, RW_LORA, RW_LORA, WIDTH,
              RET_QK, RET_QK, WIDTH, WIDTH,
              WIDTH + 4 * SSD_STATE, WIDTH, SSD_HEADS,
              3 * WIDTH, WIDTH, GDN_HEADS, GDN_HEADS,
              4 * D_MODEL)
    offs = np.cumsum((0,) + widths)
    seg = [w_in[:, int(offs[i]):int(offs[i + 1])] for i in range(len(widths))]
    (rw_r, rw_k, rw_v, rw_wlo, rw_alo, rw_z, rt_q, rt_k, rt_v, rt_z,
     sd_xbc, sd_z, sd_dt, gd_qkv, gd_z, gd_b, gd_a, gates) = seg

    def swap_pairs(w):
        w2 = w.reshape(w.shape[0], -1, 2)
        return jnp.stack([-w2[..., 1], w2[..., 0]], axis=-1).reshape(w.shape)

    w_rw = jnp.concatenate([rw_r, rw_k, rw_v, rw_wlo, rw_alo, rw_z], axis=1)
    w_rt = jnp.concatenate([rt_q, rt_k, swap_pairs(rt_q), swap_pairs(rt_k), rt_v, rt_z], axis=1)
    w_sd = jnp.concatenate([sd_xbc, sd_z, _pad_cols(sd_dt, LANE)], axis=1)
    w_gd = jnp.concatenate([gd_qkv, gd_z, _pad_cols(jnp.concatenate([gd_b, gd_a], axis=1), LANE)],
                           axis=1)
    return [w.astype(BF16) for w in (w_rw, w_rt, w_sd, w_gd, gates)]


def _head_expand(n_heads, width, offset=0):
    e = np.zeros((LANE, n_heads * width), np.float32)
    for h in range(n_heads):
        e[offset + h, h * width:(h + 1) * width] = 1.0
    return jnp.asarray(e)


def _row(v, n=None):
    v = v.reshape(1, -1).astype(F32)
    return v if n is None else _pad_cols(v, n)


def kernel(x, norm_w, w_in, rwkv_mu_rkv, rwkv_mu_wa, rwkv_w_up, rwkv_w0, rwkv_a_up, rwkv_a0,
           rwkv_k_k, rwkv_k_a, rwkv_r_k, rwkv_ln_w, rwkv_ln_b, ret_norm_w, ssd_conv_w, ssd_conv_b,
           ssd_dt_bias, ssd_A_log, ssd_D, ssd_norm_w, gdn_conv_w, gdn_dt_bias, gdn_A_log, gdn_norm_w,
           w_branch, w_out, final_norm_w):
    b, s, d = x.shape
    depth = norm_w.shape[0]
    tokens = b * s
    ret_tables = _ret_tables(s)
    e64 = _head_expand(SSD_HEADS, 64)
    e128_ssd = _head_expand(SSD_HEADS, LANE)
    e_beta = _head_expand(GDN_HEADS, LANE, 0)
    e_g = _head_expand(GDN_HEADS, LANE, GDN_HEADS)
    zeros_lora = jnp.zeros((RW_LORA, WIDTH), F32)
    x2 = x.reshape(tokens, d)
    for l in range(depth):
        w_rw, w_rt, w_sd, w_gd, w_gate = _split_w_in(w_in[l])
        nw = _row(norm_w[l])
        p_rw = _project(x2, nw, w_rw).reshape(b, s, -1)
        p_rt = _project(x2, nw, w_rt).reshape(b, s, -1)
        p_sd = _project(x2, nw, w_sd).reshape(b, s, -1)
        p_gd = _project(x2, nw, w_gd).reshape(b, s, -1)

        mu = jnp.concatenate([rwkv_mu_rkv[l].reshape(1, -1), rwkv_mu_wa[l].reshape(1, -1)], axis=1)
        w_up = jnp.concatenate([rwkv_w_up[l], zeros_lora], axis=0).astype(BF16)
        a_up = jnp.concatenate([zeros_lora, rwkv_a_up[l]], axis=0).astype(BF16)
        u_a = _rwkv_mixer(p_rw, mu, w_up, _row(rwkv_w0[l]), a_up, _row(rwkv_a0[l]),
                          _row(rwkv_k_k[l]), _row(rwkv_k_a[l]), _row(rwkv_r_k[l]),
                          _row(rwkv_ln_w[l]), _row(rwkv_ln_b[l]))
        u_b = _ret_mixer(p_rt, ret_tables, _row(ret_norm_w[l]))
        u_c = _ssd_mixer(p_sd, ssd_conv_w[l], _row(ssd_conv_b[l]), _row(ssd_dt_bias[l], LANE),
                         _row(-jnp.exp(ssd_A_log[l].astype(F32)), LANE),
                         _row(jnp.repeat(ssd_D[l], 64)), _row(ssd_norm_w[l]), e64, e128_ssd)
        gdn_bias = jnp.concatenate([jnp.zeros((GDN_HEADS,), F32), gdn_dt_bias[l]])
        gdn_a = jnp.concatenate([jnp.zeros((GDN_HEADS,), F32), -jnp.exp(gdn_A_log[l].astype(F32))])
        u_d = _gdn_mixer(p_gd, gdn_conv_w[l], _row(gdn_bias, LANE), _row(gdn_a, LANE),
                         _row(gdn_norm_w[l]), e_beta, e_g)
        us = [u.reshape(tokens, WIDTH) for u in (u_a, u_b, u_c, u_d)]
        x2 = _merge(x2, nw, w_gate, us, w_branch[l].astype(BF16), w_out[l].astype(BF16),
                    _row(final_norm_w), final_norm=(l == depth - 1))
    return x2.reshape(b, s, d)
```

```python
import functools
import math

import numpy as np
import jax
import jax.numpy as jnp
from jax import lax
from jax.experimental import pallas as pl
from jax.experimental.pallas import tpu as pltpu

F32 = jnp.float32
BF16 = jnp.bfloat16
HIGHEST = lax.Precision.HIGHEST

D_MODEL = 1024
WIDTH = 512
NORM_EPS = 1e-6
CONV_K = 4
RW_HEAD = 64
RW_LORA = 64
RW_GN_EPS = 64e-5
RET_HEADS = 8
RET_QK = 256
ROPE_BASE = 10000.0
SSD_HEADS = 8
SSD_STATE = 128
GDN_HEADS = 4
GDN_HEAD = 128

LANE = 128
SUBLANE = 8
CHUNK = 64
SSD_CHUNK = 128
TILE = 256
PROJ_TILE = 512
VMEM_LIMIT = 56 * 1024 * 1024


def _dot(a, b):
    return jnp.dot(a.astype(BF16), b.astype(BF16), preferred_element_type=F32)


def _dot_nt(a, b):
    return lax.dot_general(a.astype(BF16), b.astype(BF16), (((1,), (1,)), ((), ())),
                           preferred_element_type=F32)


def _dot_tn(a, b):
    return lax.dot_general(a.astype(BF16), b.astype(BF16), (((0,), (0,)), ((), ())),
                           preferred_element_type=F32)


def _dot_f32(a, b):
    return jnp.dot(a, b, preferred_element_type=F32, precision=HIGHEST)


def _sigmoid(x):
    return 1.0 / (1.0 + jnp.exp(-x))


def _silu(x):
    return x * _sigmoid(x)


def _softplus(x):
    return jnp.maximum(x, 0.0) + jnp.log(1.0 + jnp.exp(-jnp.abs(x)))


def _iota(shape, dim):
    return lax.broadcasted_iota(jnp.int32, shape, dim)


def _lane_mask(width, lo, hi):
    lane = _iota((1, width), 1)
    return ((lane >= lo) & (lane < hi)).astype(F32)


def _tri_chunk(n, chunk):
    r = _iota((n, n), 0)
    c = _iota((n, n), 1)
    sh = int(math.log2(chunk))
    return ((r >= c) & (jnp.right_shift(r, sh) == jnp.right_shift(c, sh))).astype(F32)


def _seg64_sum(x):
    outs = []
    for j in range(x.shape[1] // LANE):
        xb = x[:, LANE * j:LANE * (j + 1)]
        lo = _iota(xb.shape, 1) < 64
        s_lo = jnp.sum(jnp.where(lo, xb, 0.0), axis=-1, keepdims=True)
        s_hi = jnp.sum(jnp.where(lo, 0.0, xb), axis=-1, keepdims=True)
        outs.append(jnp.where(lo, s_lo, s_hi))
    return outs[0] if len(outs) == 1 else jnp.concatenate(outs, axis=-1)


def _shifted(hist_ref, first, cur, n_shift):
    tile = cur.shape[0]

    @pl.when(first)
    def _():
        hist_ref[0:SUBLANE, :] = jnp.zeros((SUBLANE, cur.shape[1]), F32)

    hist_ref[SUBLANE:SUBLANE + tile, :] = cur
    outs = [hist_ref[SUBLANE - j:SUBLANE - j + tile, :] for j in range(1, n_shift + 1)]
    hist_ref[0:SUBLANE, :] = cur[tile - SUBLANE:tile, :]
    return outs


def _causal_conv(hist_ref, first, cur, w_ref):
    d1, d2, d3 = _shifted(hist_ref, first, cur, CONV_K - 1)
    return (cur * w_ref[3:4, :] + d1 * w_ref[2:3, :] + d2 * w_ref[1:2, :] + d3 * w_ref[0:1, :])


def _neumann_inverses(ns):
    size = ns[0].shape[0]
    eye = (_iota((size, size), 0) == _iota((size, size), 1)).astype(F32)
    ps = [eye + n for n in ns]
    ns = [_dot(n, n) for n in ns]
    levels = int(math.log2(CHUNK)) - 1
    for _ in range(levels - 1):
        boths = [_dot(jnp.concatenate([p, n], axis=0), n) for p, n in zip(ps, ns)]
        ps = [p + both[:size] for p, both in zip(ps, boths)]
        ns = [both[size:] for both in boths]
    return [p + _dot(p, n) for p, n in zip(ps, ns)]


def _proj_kernel(x_ref, nw_ref, w_ref, o_ref):
    x = x_ref[...]
    h = x * lax.rsqrt(jnp.mean(x * x, axis=-1, keepdims=True) + NORM_EPS) * nw_ref[...]
    o_ref[...] = jnp.dot(h.astype(BF16), w_ref[...], preferred_element_type=F32)


def _project(x2, norm_w, w):
    tokens, d = x2.shape
    n = w.shape[1]
    return pl.pallas_call(
        _proj_kernel,
        grid=(tokens // PROJ_TILE,),
        in_specs=[pl.BlockSpec((PROJ_TILE, d), lambda i: (i, 0)),
                  pl.BlockSpec((1, d), lambda i: (0, 0)),
                  pl.BlockSpec((d, n), lambda i: (0, 0))],
        out_specs=pl.BlockSpec((PROJ_TILE, n), lambda i: (i, 0)),
        out_shape=jax.ShapeDtypeStruct((tokens, n), F32),
        name="in_proj",
        compiler_params=pltpu.CompilerParams(dimension_semantics=("arbitrary",),
                                             vmem_limit_bytes=VMEM_LIMIT),
    )(x2, norm_w, w)


def _ret_kernel(p_ref, cos_ref, sin_ref, dmat_ref, qdec_ref, ktail_ref, gam_ref, nw_ref,
                o_ref, state_ref):
    first = pl.program_id(1) == 0

    @pl.when(first)
    def _():
        state_ref[...] = jnp.zeros(state_ref.shape, F32)

    cos = cos_ref[...]
    sin = sin_ref[...]
    q = p_ref[:, 0:256] * cos + p_ref[:, 512:768] * sin
    k = (p_ref[:, 256:512] * cos + p_ref[:, 768:1024] * sin) * (32.0 ** -0.5)
    v = p_ref[:, 1024:1536]
    z = p_ref[:, 1536:2048]
    tile = q.shape[0]
    qmasks = [_lane_mask(LANE, 32 * h, 32 * h + 32) for h in range(4)]
    vmasks = [_lane_mask(256, 64 * h, 64 * h + 64) for h in range(4)]
    y_groups = []
    for g in range(2):
        qg = q[:, LANE * g:LANE * (g + 1)]
        kg = k[:, LANE * g:LANE * (g + 1)]
        vg = v[:, 256 * g:256 * (g + 1)]
        state = state_ref[g]
        dmat = dmat_ref[g]
        qdec = qdec_ref[g]
        ktail = ktail_ref[g]
        gam = gam_ref[g]
        ys = []
        for c in range(tile // CHUNK):
            rows = slice(c * CHUNK, (c + 1) * CHUNK)
            q_st = jnp.concatenate([qg[rows] * m for m in qmasks], axis=0)
            k_st = jnp.concatenate([kg[rows] * m for m in qmasks], axis=0)
            v_st = jnp.concatenate([vg[rows] * m for m in vmasks], axis=0)
            s = _dot_nt(q_st, k_st) * dmat
            y_st = _dot(s, v_st) + _dot(q_st * qdec, state)
            state = gam * state + _dot_tn(k_st * ktail, v_st)
            ys.append(y_st[0:CHUNK] + y_st[CHUNK:2 * CHUNK]
                      + y_st[2 * CHUNK:3 * CHUNK] + y_st[3 * CHUNK:4 * CHUNK])
        state_ref[g] = state
        y_groups.append(jnp.concatenate(ys, axis=0))
    y = jnp.concatenate(y_groups, axis=-1)
    ms = _seg64_sum(y * y) * (1.0 / 64.0)
    y = y * lax.rsqrt(ms + NORM_EPS) * nw_ref[...]
    o_ref[...] = (y * _silu(z)).astype(o_ref.dtype)


def _ret_tables(seq):
    half = 16
    angle = 1.0 / (ROPE_BASE ** jnp.linspace(0.0, 1.0, half, dtype=F32))
    theta = jnp.arange(seq, dtype=F32)[:, None] * angle[None, :]
    cos = jnp.tile(jnp.repeat(jnp.cos(theta), 2, axis=1), (1, RET_HEADS))
    sin = jnp.tile(jnp.repeat(jnp.sin(theta), 2, axis=1), (1, RET_HEADS))
    log_gamma = jnp.log(1.0 - jnp.exp2(-5.0 - jnp.arange(RET_HEADS, dtype=F32)))
    lg = log_gamma.reshape(2, 4)
    i = jnp.arange(CHUNK, dtype=F32)
    diff = i[:, None] - i[None, :]
    low = diff >= 0
    blocks = jnp.exp(jnp.where(low[None, None], lg[:, :, None, None] * diff[None, None], -jnp.inf))
    eye4 = jnp.eye(4, dtype=F32)
    dmat = jnp.einsum('ghij,hk->ghikj', blocks, eye4).reshape(2, 4 * CHUNK, 4 * CHUNK)
    qdec = jnp.exp(lg[:, :, None] * (i[None, None, :] + 1.0)).reshape(2, 4 * CHUNK, 1)
    qdec = jnp.broadcast_to(qdec, (2, 4 * CHUNK, LANE))
    ktail = jnp.exp(lg[:, :, None] * (CHUNK - 1.0 - i[None, None, :])).reshape(2, 4 * CHUNK, 1)
    ktail = jnp.broadcast_to(ktail, (2, 4 * CHUNK, LANE))
    gam_rows = jnp.repeat(jnp.exp(lg * CHUNK), 32, axis=1)
    gam = jnp.broadcast_to(gam_rows[:, :, None], (2, LANE, 256))
    return cos, sin, dmat, qdec, ktail, gam


def _ret_mixer(p, tables, norm_w):
    b, s, n = p.shape
    cos, sin, dmat, qdec, ktail, gam = tables
    full = lambda *shape: pl.BlockSpec(shape, lambda bi, i: (0,) * len(shape))
    return pl.pallas_call(
        _ret_kernel,
        grid=(b, s // TILE),
        in_specs=[pl.BlockSpec((None, TILE, n), lambda bi, i: (bi, i, 0)),
                  pl.BlockSpec((TILE, 256), lambda bi, i: (i, 0)),
                  pl.BlockSpec((TILE, 256), lambda bi, i: (i, 0)),
                  full(2, 4 * CHUNK, 4 * CHUNK), full(2, 4 * CHUNK, LANE), full(2, 4 * CHUNK, LANE),
                  full(2, LANE, 256), full(1, WIDTH)],
        out_specs=pl.BlockSpec((None, TILE, WIDTH), lambda bi, i: (bi, i, 0)),
        out_shape=jax.ShapeDtypeStruct((b, s, WIDTH), BF16),
        scratch_shapes=[pltpu.VMEM((2, LANE, 256), F32)],
        name="ret_mixer",
        compiler_params=pltpu.CompilerParams(dimension_semantics=("arbitrary", "arbitrary"),
                                             vmem_limit_bytes=VMEM_LIMIT),
    )(p, cos, sin, dmat, qdec, ktail, gam, norm_w)


def _ssd_kernel(p_ref, cw_ref, cb_ref, dtb_ref, a_ref, dskip_ref, nw_ref, e64_ref, e128_ref,
                o_ref, hist_ref, state_ref):
    first = pl.program_id(1) == 0

    @pl.when(first)
    def _():
        state_ref[...] = jnp.zeros(state_ref.shape, F32)

    xbc = _silu(_causal_conv(hist_ref, first, p_ref[:, 0:1024], cw_ref) + cb_ref[...])
    z = p_ref[:, 1024:1536]
    dt = _softplus(p_ref[:, 1536:1664] + dtb_ref[...])
    log_a = dt * a_ref[...]
    tile = xbc.shape[0]
    xs = xbc[:, 0:512]
    g_cum = _dot_f32(_tri_chunk(tile, SSD_CHUNK), log_a)
    g_all = _dot_f32(g_cum, e128_ref[...])
    val = xs * _dot_f32(dt, e64_ref[...])
    ri = _iota((SSD_CHUNK, SSD_CHUNK), 0)
    ci = _iota((SSD_CHUNK, SSD_CHUNK), 1)
    low = ri >= ci
    hmask = [_lane_mask(256, 64 * h, 64 * h + 64) for h in range(4)]
    y_groups = []
    for g in range(2):
        bm = xbc[:, 512 + LANE * g:512 + LANE * (g + 1)]
        cm = xbc[:, 768 + LANE * g:768 + LANE * (g + 1)]
        vg = val[:, 256 * g:256 * (g + 1)]
        ys = []
        for c in range(tile // SSD_CHUNK):
            rows = slice(c * SSD_CHUNK, (c + 1) * SSD_CHUNK)
            qk = _dot_nt(cm[rows], bm[rows])
            vc = vg[rows]
            s_cat, q_cat = [], []
            for h in range(4):
                head = 4 * g + h
                gc = g_all[rows, LANE * head:LANE * (head + 1)]
                dec = jnp.exp(jnp.where(low, gc - gc.T, -jnp.inf))
                s_cat.append(qk * dec)
                q_cat.append(cm[rows] * jnp.exp(gc))
            v_st = jnp.concatenate([vc * m for m in hmask], axis=0)
            y = _dot(jnp.concatenate(s_cat, axis=-1), v_st)
            y = y + _dot(jnp.concatenate(q_cat, axis=-1), state_ref[g])
            for h in range(4):
                head = 4 * g + h
                gc = g_all[rows, LANE * head:LANE * (head + 1)]
                g_last = gc[SSD_CHUNK - 1:SSD_CHUNK, :]
                k_tail = bm[rows] * jnp.exp(g_last - gc)
                upd = _dot_tn(k_tail, vc) * hmask[h]
                lam = jnp.concatenate([jnp.exp(g_last), jnp.exp(g_last)], axis=-1)
                blk = slice(SSD_STATE * h, SSD_STATE * (h + 1))
                state_ref[g, blk, :] = state_ref[g, blk, :] * lam + upd
            ys.append(y)
        y_groups.append(jnp.concatenate(ys, axis=0))
    y = jnp.concatenate(y_groups, axis=-1) + xs * dskip_ref[...]
    y = y * _silu(z)
    outs = []
    for g in range(2):
        yg = y[:, 256 * g:256 * (g + 1)]
        ms = jnp.sum(yg * yg, axis=-1, keepdims=True) * (1.0 / 256.0)
        outs.append(yg * lax.rsqrt(ms + NORM_EPS))
    o_ref[...] = (jnp.concatenate(outs, axis=-1) * nw_ref[...]).astype(o_ref.dtype)


def _ssd_mixer(p, conv_w, conv_b, dt_bias, a_vec, d_skip, norm_w, e64, e128):
    b, s, n = p.shape
    full = lambda *shape: pl.BlockSpec(shape, lambda bi, i: (0,) * len(shape))
    return pl.pallas_call(
        _ssd_kernel,
        grid=(b, s // TILE),
        in_specs=[pl.BlockSpec((None, TILE, n), lambda bi, i: (bi, i, 0)),
                  full(CONV_K, 1024), full(1, 1024), full(1, LANE), full(1, LANE),
                  full(1, WIDTH), full(1, WIDTH), full(LANE, WIDTH), full(LANE, SSD_HEADS * LANE)],
        out_specs=pl.BlockSpec((None, TILE, WIDTH), lambda bi, i: (bi, i, 0)),
        out_shape=jax.ShapeDtypeStruct((b, s, WIDTH), BF16),
        scratch_shapes=[pltpu.VMEM((SUBLANE + TILE, 1024), F32),
                        pltpu.VMEM((2, 4 * SSD_STATE, 256), F32)],
        name="ssd_mixer",
        compiler_params=pltpu.CompilerParams(dimension_semantics=("arbitrary", "arbitrary"),
                                             vmem_limit_bytes=VMEM_LIMIT),
    )(p, conv_w, conv_b, dt_bias, a_vec, d_skip, norm_w, e64, e128)


def _gdn_kernel(p_ref, cw_ref, dtb_ref, a_ref, nw_ref, eb_ref, eg_ref,
                o_ref, hist_ref, state_ref):
    first = pl.program_id(1) == 0

    @pl.when(first)
    def _():
        state_ref[...] = jnp.zeros(state_ref.shape, F32)

    qkv = _silu(_causal_conv(hist_ref, first, p_ref[:, 0:1536], cw_ref))
    z = p_ref[:, 1536:2048]
    ba = p_ref[:, 2048:2176]
    beta_blk = _sigmoid(ba)
    log_a = a_ref[...] * _softplus(ba + dtb_ref[...])
    tile = qkv.shape[0]
    g_all = _dot_f32(_dot_f32(_tri_chunk(tile, CHUNK), log_a), eg_ref[...])
    beta_all = _dot_f32(beta_blk, eb_ref[...])

    def l2n(t):
        return t * lax.rsqrt(jnp.sum(t * t, axis=-1, keepdims=True) + 1e-6)

    two = 2 * CHUNK
    ri = _iota((two, two), 0)
    ci = _iota((two, two), 1)
    same = jnp.right_shift(ri, 6) == jnp.right_shift(ci, 6)
    incl = same & (ri >= ci)
    strict = same & (ri > ci)
    zero = jnp.zeros((CHUNK, GDN_HEAD), F32)
    n_chunks = tile // CHUNK
    items = [(c, pr) for c in range(n_chunks) for pr in range(2)]

    def stack(t0, t1):
        return jnp.concatenate([jnp.concatenate([t0, zero], axis=-1),
                                jnp.concatenate([zero, t1], axis=-1)], axis=0)

    qh = [l2n(qkv[:, LANE * h:LANE * (h + 1)]) * (GDN_HEAD ** -0.5) for h in range(GDN_HEADS)]
    kh = [l2n(qkv[:, 512 + LANE * h:512 + LANE * (h + 1)]) for h in range(GDN_HEADS)]
    vh = [qkv[:, 1024 + LANE * h:1024 + LANE * (h + 1)] for h in range(GDN_HEADS)]

    q_dec, attn, k_tail, lam, ms, rhs = {}, {}, {}, {}, [], {}
    for c, pr in items:
        rows = slice(c * CHUNK, (c + 1) * CHUNK)
        h0, h1 = 2 * pr, 2 * pr + 1
        gc = jnp.concatenate([g_all[rows, LANE * h:LANE * (h + 1)] for h in (h0, h1)], axis=0)
        bt = jnp.concatenate([beta_all[rows, LANE * h:LANE * (h + 1)] for h in (h0, h1)], axis=0)
        gc2 = jnp.concatenate([gc, gc], axis=-1)
        bt2 = jnp.concatenate([bt, bt], axis=-1)
        q_st = stack(qh[h0][rows], qh[h1][rows])
        k_st = stack(kh[h0][rows], kh[h1][rows])
        v_st = stack(vh[h0][rows], vh[h1][rows])
        kb_st = k_st * bt2
        dec_i = jnp.exp(jnp.where(incl, gc - gc.T, -jnp.inf))
        sc = _dot_nt(jnp.concatenate([kb_st, q_st], axis=0), k_st)
        ms.append(-jnp.where(strict, sc[0:two] * dec_i, 0.0))
        attn[c, pr] = sc[two:2 * two] * dec_i
        eg = jnp.exp(gc2)
        rhs[c, pr] = jnp.concatenate([v_st * bt2, kb_st * eg], axis=-1)
        q_dec[c, pr] = q_st * eg
        g_last = jnp.concatenate(
            [jnp.broadcast_to(gc2[CHUNK - 1:CHUNK, :], (CHUNK, 256)),
             jnp.broadcast_to(gc2[two - 1:two, :], (CHUNK, 256))], axis=0)
        k_tail[c, pr] = k_st * jnp.exp(g_last - gc2)
        lam[c, pr] = jnp.exp(jnp.concatenate([gc[CHUNK - 1:CHUNK, :], gc[two - 1:two, :]], axis=-1))
    t_invs = _neumann_inverses(ms)
    o_q, o_0, p_t, q_t = {}, {}, {}, {}
    for it, t in zip(items, t_invs):
        uw = _dot(t, rhs[it])
        au = _dot(attn[it], uw)
        o_0[it] = au[:, 0:256]
        o_q[it] = q_dec[it] - au[:, 256:512]
        pq = _dot_tn(jnp.concatenate([-uw[:, 256:512], uw[:, 0:256]], axis=-1), k_tail[it])
        p_t[it] = pq[0:256]
        q_t[it] = pq[256:512]

    states = [state_ref[pr] for pr in range(2)]
    os = [[], []]
    for c in range(n_chunks):
        for pr in range(2):
            o_st = _dot_nt(o_q[c, pr], states[pr]) + o_0[c, pr]
            states[pr] = states[pr] * lam[c, pr] + _dot(states[pr], p_t[c, pr]) + q_t[c, pr]
            os[pr].append(o_st[0:CHUNK] + o_st[CHUNK:two])
    o_pairs = []
    for pr in range(2):
        state_ref[pr] = states[pr]
        o_pairs.append(jnp.concatenate(os[pr], axis=0))
    outs = []
    for pr in range(2):
        for j in range(2):
            oh = o_pairs[pr][:, LANE * j:LANE * (j + 1)]
            ms = jnp.sum(oh * oh, axis=-1, keepdims=True) * (1.0 / GDN_HEAD)
            outs.append(oh * lax.rsqrt(ms + NORM_EPS) * nw_ref[...])
    o_ref[...] = (jnp.concatenate(outs, axis=-1) * _silu(z)).astype(o_ref.dtype)


def _gdn_mixer(p, conv_w, dt_bias, a_vec, norm_w, e_beta, e_g):
    b, s, n = p.shape
    full = lambda *shape: pl.BlockSpec(shape, lambda bi, i: (0,) * len(shape))
    return pl.pallas_call(
        _gdn_kernel,
        grid=(b, s // TILE),
        in_specs=[pl.BlockSpec((None, TILE, n), lambda bi, i: (bi, i, 0)),
                  full(CONV_K, 1536), full(1, LANE), full(1, LANE), full(1, GDN_HEAD),
                  full(LANE, GDN_HEADS * LANE), full(LANE, GDN_HEADS * LANE)],
        out_specs=pl.BlockSpec((None, TILE, WIDTH), lambda bi, i: (bi, i, 0)),
        out_shape=jax.ShapeDtypeStruct((b, s, WIDTH), BF16),
        scratch_shapes=[pltpu.VMEM((SUBLANE + TILE, 1536), F32),
                        pltpu.VMEM((2, 256, 256), F32)],
        name="gdn_mixer",
        compiler_params=pltpu.CompilerParams(dimension_semantics=("arbitrary", "arbitrary"),
                                             vmem_limit_bytes=VMEM_LIMIT),
    )(p, conv_w, dt_bias, a_vec, norm_w, e_beta, e_g)


def _rwkv_kernel(p_ref, mu_ref, wup_ref, w0_ref, aup_ref, a0_ref, kk_ref, ka_ref,
                 rk_ref, lnw_ref, lnb_ref, o_ref, hist_ref, state_ref):
    first = pl.program_id(1) == 0

    @pl.when(first)
    def _():
        state_ref[...] = jnp.zeros(state_ref.shape, F32)

    cur = p_ref[:, 0:1664]
    (prev,) = _shifted(hist_ref, first, cur, 1)
    mixed = cur + (prev - cur) * mu_ref[...]
    r = mixed[:, 0:512]
    k = mixed[:, 512:1024]
    v = mixed[:, 1024:1536]
    lo = mixed[:, 1536:1664]
    z = p_ref[:, 1664:2176]
    tile = r.shape[0]
    w_log = -_softplus(-(w0_ref[...] + _dot(jnp.tanh(lo), wup_ref[...]))) - 0.5
    logw = -jnp.exp(w_log)
    iclr = _sigmoid(a0_ref[...] + _dot(lo, aup_ref[...]))
    kkr = k * kk_ref[...]
    kk = kkr * lax.rsqrt(_seg64_sum(kkr * kkr) + 1e-6)
    k = k * (1.0 + (iclr - 1.0) * ka_ref[...])
    a = -kk
    b = kk * iclr
    g_inc = _dot_f32(_tri_chunk(tile, CHUNK), logw)
    g_exc = g_inc - logw
    e_inc = jnp.exp(g_inc)
    e_neg = jnp.exp(-g_inc)
    r_t = r * e_inc
    a_t = a * jnp.exp(g_exc)
    k_h = k * e_neg
    b_h = b * e_neg

    two = 2 * CHUNK
    ri = _iota((two, two), 0)
    ci = _iota((two, two), 1)
    same = jnp.right_shift(ri, 6) == jnp.right_shift(ci, 6)
    incl = (same & (ri >= ci)).astype(F32)
    strict = (same & (ri > ci)).astype(F32)
    m0 = _lane_mask(LANE, 0, 64)
    m1 = _lane_mask(LANE, 64, LANE)

    def stack(t):
        return jnp.concatenate([t * m0, t * m1], axis=0)

    n_chunks = tile // CHUNK
    n_pairs = WIDTH // LANE
    items = [(c, pr) for c in range(n_chunks) for pr in range(n_pairs)]

    ar_st, v_st, kbt_st, e_last, a_rb, a_abs, av = {}, {}, {}, {}, {}, [], {}
    for c, pr in items:
        rows = slice(c * CHUNK, (c + 1) * CHUNK)
        lanes = slice(LANE * pr, LANE * (pr + 1))
        ar_st[c, pr] = jnp.concatenate([stack(a_t[rows, lanes]), stack(r_t[rows, lanes])], axis=0)
        kb_st = jnp.concatenate([stack(k_h[rows, lanes]), stack(b_h[rows, lanes])], axis=0)
        v_st[c, pr] = stack(v[rows, lanes])
        sc = _dot_nt(ar_st[c, pr], kb_st)
        a_abs.append(sc[0:two, two:2 * two] * strict)
        a_rb[c, pr] = sc[two:2 * two, two:2 * two] * incl
        a_akrk = jnp.concatenate([sc[0:two, 0:two] * strict, sc[two:2 * two, 0:two] * incl], axis=0)
        av[c, pr] = _dot(a_akrk, v_st[c, pr])
        g_last = g_inc[(c + 1) * CHUNK - 1:(c + 1) * CHUNK, lanes]
        tail = jnp.exp(g_last - g_inc[rows, lanes])
        kbt_st[c, pr] = jnp.concatenate([stack(k[rows, lanes] * tail), stack(b[rows, lanes] * tail)],
                                        axis=0)
        e_last[c, pr] = jnp.exp(g_last)
    t_invs = _neumann_inverses(a_abs)
    r_q, y_0, p_l, q_l = {}, {}, {}, {}
    for it, t in zip(items, t_invs):
        wu = _dot(t, jnp.concatenate([ar_st[it][0:two], av[it][0:two]], axis=-1))
        ry = jnp.concatenate([ar_st[it][two:2 * two], av[it][two:2 * two]], axis=-1) + _dot(a_rb[it], wu)
        r_q[it] = ry[:, 0:LANE]
        y_0[it] = ry[:, LANE:2 * LANE]
        p_l[it] = _dot_tn(wu[:, 0:LANE], kbt_st[it][two:2 * two])
        q_l[it] = _dot_tn(jnp.concatenate([v_st[it], wu[:, LANE:2 * LANE]], axis=0), kbt_st[it])

    states = [state_ref[pr] for pr in range(n_pairs)]
    ys = [[] for _ in range(n_pairs)]
    for c in range(n_chunks):
        for pr in range(n_pairs):
            y_st = _dot_nt(r_q[c, pr], states[pr]) + y_0[c, pr]
            states[pr] = states[pr] * e_last[c, pr] + _dot(states[pr], p_l[c, pr]) + q_l[c, pr]
            ys[pr].append(y_st[0:CHUNK] + y_st[CHUNK:two])
    for pr in range(n_pairs):
        state_ref[pr] = states[pr]
    y = jnp.concatenate([jnp.concatenate(ys[pr], axis=0) for pr in range(n_pairs)], axis=-1)
    mu = _seg64_sum(y) * (1.0 / RW_HEAD)
    yc = y - mu
    var = _seg64_sum(yc * yc) * (1.0 / RW_HEAD)
    yn = yc * lax.rsqrt(var + RW_GN_EPS) * lnw_ref[...] + lnb_ref[...]
    bonus = _seg64_sum(r * k * rk_ref[...]) * v
    o_ref[...] = ((yn + bonus) * _silu(z)).astype(o_ref.dtype)


def _rwkv_mixer(p, mu, w_up, w0, a_up, a0, k_k, k_a, r_k, ln_w, ln_b):
    b, s, n = p.shape
    full = lambda *shape: pl.BlockSpec(shape, lambda bi, i: (0,) * len(shape))
    return pl.pallas_call(
        _rwkv_kernel,
        grid=(b, s // TILE),
        in_specs=[pl.BlockSpec((None, TILE, n), lambda bi, i: (bi, i, 0)),
                  full(1, 1664)] + [full(LANE, WIDTH), full(1, WIDTH)] * 2
                 + [full(1, WIDTH)] * 5,
        out_specs=pl.BlockSpec((None, TILE, WIDTH), lambda bi, i: (bi, i, 0)),
        out_shape=jax.ShapeDtypeStruct((b, s, WIDTH), BF16),
        scratch_shapes=[pltpu.VMEM((SUBLANE + TILE, 1664), F32),
                        pltpu.VMEM((4, LANE, LANE), F32)],
        name="rwkv_mixer",
        compiler_params=pltpu.CompilerParams(dimension_semantics=("arbitrary", "arbitrary"),
                                             vmem_limit_bytes=VMEM_LIMIT),
    )(p, mu, w_up, w0, a_up, a0, k_k, k_a, r_k, ln_w, ln_b)


def _merge_kernel(x_ref, nw_ref, wg_ref, ua_ref, ub_ref, uc_ref, ud_ref, wb_ref, wo_ref, fw_ref,
                  o_ref, *, final_norm):
    x = x_ref[...]
    h = (x * lax.rsqrt(jnp.mean(x * x, axis=-1, keepdims=True) + NORM_EPS) * nw_ref[...]).astype(BF16)
    merged = None
    for i, u_ref in enumerate((ua_ref, ub_ref, uc_ref, ud_ref)):
        gate = _sigmoid(jnp.dot(h, wg_ref[:, D_MODEL * i:D_MODEL * (i + 1)],
                                preferred_element_type=F32))
        term = gate * jnp.dot(u_ref[...], wb_ref[i], preferred_element_type=F32)
        merged = term if merged is None else merged + term
    out = x + jnp.dot(merged.astype(BF16), wo_ref[...], preferred_element_type=F32)
    if final_norm:
        out = out * lax.rsqrt(jnp.mean(out * out, axis=-1, keepdims=True) + NORM_EPS) * fw_ref[...]
    o_ref[...] = out


def _merge(x2, norm_w, w_gate, us, w_branch, w_out, final_w, final_norm):
    tokens, d = x2.shape
    row = lambda n: pl.BlockSpec((PROJ_TILE, n), lambda i: (i, 0))
    const = lambda *shape: pl.BlockSpec(shape, lambda i: (0,) * len(shape))
    return pl.pallas_call(
        functools.partial(_merge_kernel, final_norm=final_norm),
        grid=(tokens // PROJ_TILE,),
        in_specs=[row(d), const(1, d), const(d, 4 * d)] + [row(WIDTH)] * 4
                 + [const(4, WIDTH, d), const(d, d), const(1, d)],
        out_specs=row(d),
        out_shape=jax.ShapeDtypeStruct((tokens, d), F32),
        name="merge",
        compiler_params=pltpu.CompilerParams(dimension_semantics=("arbitrary",),
                                             vmem_limit_bytes=VMEM_LIMIT),
    )(x2, norm_w, w_gate, *us, w_branch, w_out, final_w)


def _pad_cols(w, n):
    return jnp.pad(w, ((0, 0), (0, n - w.shape[1])))


def _split_w_in(w_in):
    widths = (WIDTH, WIDTH, WIDTH, RW_LORA, RW_LORA, WIDTH,
              RET_QK, RET_QK, WIDTH, WIDTH,
              WIDTH + 4 * SSD_STATE, WIDTH, SSD_HEADS,
              3 * WIDTH, WIDTH, GDN_HEADS, GDN_HEADS,
              4 * D_MODEL)
    offs = np.cumsum((0,) + widths)
    seg = [w_in[:, int(offs[i]):int(offs[i + 1])] for i in range(len(widths))]
    (rw_r, rw_k, rw_v, rw_wlo, rw_alo, rw_z, rt_q, rt_k, rt_v, rt_z,
     sd_xbc, sd_z, sd_dt, gd_qkv, gd_z, gd_b, gd_a, gates) = seg

    def swap_pairs(w):
        w2 = w.reshape(w.shape[0], -1, 2)
        return jnp.stack([-w2[..., 1], w2[..., 0]], axis=-1).reshape(w.shape)

    w_rw = jnp.concatenate([rw_r, rw_k, rw_v, rw_wlo, rw_alo, rw_z], axis=1)
    w_rt = jnp.concatenate([rt_q, rt_k, swap_pairs(rt_q), swap_pairs(rt_k), rt_v, rt_z], axis=1)
    w_sd = jnp.concatenate([sd_xbc, sd_z, _pad_cols(sd_dt, LANE)], axis=1)
    w_gd = jnp.concatenate([gd_qkv, gd_z, _pad_cols(jnp.concatenate([gd_b, gd_a], axis=1), LANE)],
                           axis=1)
    return [w.astype(BF16) for w in (w_rw, w_rt, w_sd, w_gd, gates)]


def _head_expand(n_heads, width, offset=0):
    e = np.zeros((LANE, n_heads * width), np.float32)
    for h in range(n_heads):
        e[offset + h, h * width:(h + 1) * width] = 1.0
    return jnp.asarray(e)


def _row(v, n=None):
    v = v.reshape(1, -1).astype(F32)
    return v if n is None else _pad_cols(v, n)


def kernel(x, norm_w, w_in, rwkv_mu_rkv, rwkv_mu_wa, rwkv_w_up, rwkv_w0, rwkv_a_up, rwkv_a0,
           rwkv_k_k, rwkv_k_a, rwkv_r_k, rwkv_ln_w, rwkv_ln_b, ret_norm_w, ssd_conv_w, ssd_conv_b,
           ssd_dt_bias, ssd_A_log, ssd_D, ssd_norm_w, gdn_conv_w, gdn_dt_bias, gdn_A_log, gdn_norm_w,
           w_branch, w_out, final_norm_w):
    b, s, d = x.shape
    depth = norm_w.shape[0]
    tokens = b * s
    ret_tables = _ret_tables(s)
    e64 = _head_expand(SSD_HEADS, 64)
    e128_ssd = _head_expand(SSD_HEADS, LANE)
    e_beta = _head_expand(GDN_HEADS, LANE, 0)
    e_g = _head_expand(GDN_HEADS, LANE, GDN_HEADS)
    zeros_lora = jnp.zeros((RW_LORA, WIDTH), F32)
    x2 = x.reshape(tokens, d)
    for l in range(depth):
        w_rw, w_rt, w_sd, w_gd, w_gate = _split_w_in(w_in[l])
        nw = _row(norm_w[l])
        p_rw = _project(x2, nw, w_rw).reshape(b, s, -1)
        p_rt = _project(x2, nw, w_rt).reshape(b, s, -1)
        p_sd = _project(x2, nw, w_sd).reshape(b, s, -1)
        p_gd = _project(x2, nw, w_gd).reshape(b, s, -1)

        mu = jnp.concatenate([rwkv_mu_rkv[l].reshape(1, -1), rwkv_mu_wa[l].reshape(1, -1)], axis=1)
        w_up = jnp.concatenate([rwkv_w_up[l], zeros_lora], axis=0).astype(BF16)
        a_up = jnp.concatenate([zeros_lora, rwkv_a_up[l]], axis=0).astype(BF16)
        u_a = _rwkv_mixer(p_rw, mu, w_up, _row(rwkv_w0[l]), a_up, _row(rwkv_a0[l]),
                          _row(rwkv_k_k[l]), _row(rwkv_k_a[l]), _row(rwkv_r_k[l]),
                          _row(rwkv_ln_w[l]), _row(rwkv_ln_b[l]))
        u_b = _ret_mixer(p_rt, ret_tables, _row(ret_norm_w[l]))
        u_c = _ssd_mixer(p_sd, ssd_conv_w[l], _row(ssd_conv_b[l]), _row(ssd_dt_bias[l], LANE),
                         _row(-jnp.exp(ssd_A_log[l].astype(F32)), LANE),
                         _row(jnp.repeat(ssd_D[l], 64)), _row(ssd_norm_w[l]), e64, e128_ssd)
        gdn_bias = jnp.concatenate([jnp.zeros((GDN_HEADS,), F32), gdn_dt_bias[l]])
        gdn_a = jnp.concatenate([jnp.zeros((GDN_HEADS,), F32), -jnp.exp(gdn_A_log[l].astype(F32))])
        u_d = _gdn_mixer(p_gd, gdn_conv_w[l], _row(gdn_bias, LANE), _row(gdn_a, LANE),
                         _row(gdn_norm_w[l]), e_beta, e_g)
        us = [u.reshape(tokens, WIDTH) for u in (u_a, u_b, u_c, u_d)]
        x2 = _merge(x2, nw, w_gate, us, w_branch[l].astype(BF16), w_out[l].astype(BF16),
                    _row(final_norm_w), final_norm=(l == depth - 1))
    return x2.reshape(b, s, d)
```

```python
import functools
import math

import numpy as np
import jax
import jax.numpy as jnp
from jax import lax
from jax.experimental import pallas as pl
from jax.experimental.pallas import tpu as pltpu

F32 = jnp.float32
BF16 = jnp.bfloat16

D_MODEL = 1024
WIDTH = 512
NORM_EPS = 1e-6
CONV_K = 4
RW_HEAD = 64
RW_LORA = 64
RW_GN_EPS = 64e-5
RET_HEADS = 8
RET_QK = 256
ROPE_BASE = 10000.0
SSD_HEADS = 8
SSD_STATE = 128
GDN_HEADS = 4
GDN_HEAD = 128

LANE = 128
SUBLANE = 8
CHUNK = 64
SSD_CHUNK = 128
TILE = 256
ROWS = 1
PROJ_TILE = 512
VMEM_LIMIT = 56 * 1024 * 1024


def _dot(a, b):
    return jnp.dot(a.astype(BF16), b.astype(BF16), preferred_element_type=F32)


def _dot_nt(a, b):
    return lax.dot_general(a.astype(BF16), b.astype(BF16), (((1,), (1,)), ((), ())),
                           preferred_element_type=F32)


def _dot_tn(a, b):
    return lax.dot_general(a.astype(BF16), b.astype(BF16), (((0,), (0,)), ((), ())),
                           preferred_element_type=F32)


def _bf16_pieces(x):
    hi = x.astype(BF16)
    r1 = x - hi.astype(F32)
    mid = r1.astype(BF16)
    lo = (r1 - mid.astype(F32)).astype(BF16)
    return hi, mid, lo


def _select_rows(sel, x):
    n = x.shape[1]
    y = jnp.dot(sel.astype(BF16), jnp.concatenate(_bf16_pieces(x), axis=-1),
                preferred_element_type=F32)
    return y[:, 0:n] + y[:, n:2 * n] + y[:, 2 * n:3 * n]


def _select_cols(x, sel):
    m = x.shape[0]
    y = jnp.dot(jnp.concatenate(_bf16_pieces(x), axis=0), sel.astype(BF16),
                preferred_element_type=F32)
    return y[0:m] + y[m:2 * m] + y[2 * m:3 * m]


def _sigmoid(x):
    return 1.0 / (1.0 + jnp.exp(-x))


def _silu(x):
    return x * _sigmoid(x)


def _softplus(x):
    return jnp.maximum(x, 0.0) + jnp.log(1.0 + jnp.exp(-jnp.abs(x)))


def _iota(shape, dim):
    return lax.broadcasted_iota(jnp.int32, shape, dim)


def _lane_mask(width, lo, hi):
    lane = _iota((1, width), 1)
    return ((lane >= lo) & (lane < hi)).astype(F32)


def _tri_chunk(n, chunk):
    r = _iota((n, n), 0)
    c = _iota((n, n), 1)
    sh = int(math.log2(chunk))
    return ((r >= c) & (jnp.right_shift(r, sh) == jnp.right_shift(c, sh))).astype(F32)


def _seg64_sum(x):
    outs = []
    for j in range(x.shape[1] // LANE):
        xb = x[:, LANE * j:LANE * (j + 1)]
        lo = _iota(xb.shape, 1) < 64
        s_lo = jnp.sum(jnp.where(lo, xb, 0.0), axis=-1, keepdims=True)
        s_hi = jnp.sum(jnp.where(lo, 0.0, xb), axis=-1, keepdims=True)
        outs.append(jnp.where(lo, s_lo, s_hi))
    return outs[0] if len(outs) == 1 else jnp.concatenate(outs, axis=-1)


def _shifted(hist_ref, first, cur, n_shift):
    tile = cur.shape[0]

    @pl.when(first)
    def _():
        hist_ref[0:SUBLANE, :] = jnp.zeros((SUBLANE, cur.shape[1]), F32)

    hist_ref[SUBLANE:SUBLANE + tile, :] = cur
    outs = [hist_ref[SUBLANE - j:SUBLANE - j + tile, :] for j in range(1, n_shift + 1)]
    hist_ref[0:SUBLANE, :] = cur[tile - SUBLANE:tile, :]
    return outs


def _causal_conv(hist_ref, first, cur, w_ref):
    d1, d2, d3 = _shifted(hist_ref, first, cur, CONV_K - 1)
    return (cur * w_ref[3:4, :] + d1 * w_ref[2:3, :] + d2 * w_ref[1:2, :] + d3 * w_ref[0:1, :])


def _neumann_inverses(ns):
    size = ns[0].shape[0]
    assert size == LANE and len(ns) % 2 == 0
    eye = (_iota((size, size), 0) == _iota((size, size), 1)).astype(F32)
    zero = jnp.zeros((size, size), BF16)

    def weights(row):
        row = row.astype(BF16)
        return jnp.concatenate([jnp.concatenate([row[:, 0:size], zero], axis=-1),
                                jnp.concatenate([zero, row[:, size:2 * size]], axis=-1)], axis=0)

    ns = [jnp.concatenate([ns[i], ns[i + 1]], axis=-1) for i in range(0, len(ns), 2)]
    eye2 = jnp.concatenate([eye, eye], axis=-1)
    ps = [eye2 + n for n in ns]
    ns = [_dot(n, weights(n)) for n in ns]
    levels = int(math.log2(CHUNK)) - 1
    for _ in range(levels - 1):
        boths = [_dot(jnp.concatenate([p, n], axis=0), weights(n)) for p, n in zip(ps, ns)]
        ps = [p + both[:size] for p, both in zip(ps, boths)]
        ns = [both[size:] for both in boths]
    ps = [p + _dot(p, weights(n)) for p, n in zip(ps, ns)]
    return [half for p in ps for half in (p[:, 0:size], p[:, size:2 * size])]


def _proj_kernel(x_ref, nw_ref, w_ref, o_ref):
    x = x_ref[...]
    h = x * lax.rsqrt(jnp.mean(x * x, axis=-1, keepdims=True) + NORM_EPS) * nw_ref[...]
    o_ref[...] = jnp.dot(h.astype(BF16), w_ref[...], preferred_element_type=F32)


def _project(x2, norm_w, w):
    tokens, d = x2.shape
    n = w.shape[1]
    return pl.pallas_call(
        _proj_kernel,
        grid=(tokens // PROJ_TILE,),
        in_specs=[pl.BlockSpec((PROJ_TILE, d), lambda i: (i, 0)),
                  pl.BlockSpec((1, d), lambda i: (0, 0)),
                  pl.BlockSpec((d, n), lambda i: (0, 0))],
        out_specs=pl.BlockSpec((PROJ_TILE, n), lambda i: (i, 0)),
        out_shape=jax.ShapeDtypeStruct((tokens, n), F32),
        name="in_proj",
        compiler_params=pltpu.CompilerParams(dimension_semantics=("arbitrary",),
                                             vmem_limit_bytes=VMEM_LIMIT),
    )(x2, norm_w, w)


def _ret_kernel(p_ref, cos_ref, sin_ref, dmat_ref, qdec_ref, ktail_ref, gam_ref, nw_ref,
                o_ref, state_ref):
    first = pl.program_id(1) == 0

    @pl.when(first)
    def _():
        state_ref[...] = jnp.zeros(state_ref.shape, F32)

    cos = cos_ref[...]
    sin = sin_ref[...]
    q = p_ref[:, 0:256] * cos + p_ref[:, 512:768] * sin
    k = (p_ref[:, 256:512] * cos + p_ref[:, 768:1024] * sin) * (32.0 ** -0.5)
    v = p_ref[:, 1024:1536]
    z = p_ref[:, 1536:2048]
    tile = q.shape[0]
    qmasks = [_lane_mask(LANE, 32 * h, 32 * h + 32) for h in range(4)]
    vmasks = [_lane_mask(256, 64 * h, 64 * h + 64) for h in range(4)]
    n_chunks = tile // CHUNK
    items = [(g, c) for g in range(2) for c in range(n_chunks)]

    def stacked(t, it, width, masks):
        g, c = it
        blk = t[c * CHUNK:(c + 1) * CHUNK, width * g:width * (g + 1)]
        return jnp.concatenate([blk * m for m in masks], axis=0)

    q_st = {it: stacked(q, it, LANE, qmasks) for it in items}
    k_st = {it: stacked(k, it, LANE, qmasks) for it in items}
    v_st = {it: stacked(v, it, 256, vmasks).astype(BF16) for it in items}
    q_dec = {it: (q_st[it] * qdec_ref[it[0]]).astype(BF16) for it in items}
    k_tl = {it: (k_st[it] * ktail_ref[it[0]]).astype(BF16) for it in items}
    s = {it: (_dot_nt(q_st[it], k_st[it]) * dmat_ref[it[0]]).astype(BF16) for it in items}
    y_in = {it: _dot(s[it], v_st[it]) for it in items}
    upd = {it: _dot_tn(k_tl[it], v_st[it]) for it in items}
    y_groups = []
    for g in range(2):
        state = state_ref[g]
        ys = []
        for c in range(n_chunks):
            y_st = y_in[g, c] + _dot(q_dec[g, c], state)
            state = gam_ref[g] * state + upd[g, c]
            ys.append(y_st[0:CHUNK] + y_st[CHUNK:2 * CHUNK]
                      + y_st[2 * CHUNK:3 * CHUNK] + y_st[3 * CHUNK:4 * CHUNK])
        state_ref[g] = state
        y_groups.append(jnp.concatenate(ys, axis=0))
    y = jnp.concatenate(y_groups, axis=-1)
    ms = _seg64_sum(y * y) * (1.0 / 64.0)
    y = y * lax.rsqrt(ms + NORM_EPS) * nw_ref[...]
    o_ref[...] = (y * _silu(z)).astype(o_ref.dtype)


def _ret_tables(seq):
    half = 16
    angle = 1.0 / (ROPE_BASE ** jnp.linspace(0.0, 1.0, half, dtype=F32))
    theta = jnp.arange(seq, dtype=F32)[:, None] * angle[None, :]
    cos = jnp.tile(jnp.repeat(jnp.cos(theta), 2, axis=1), (1, RET_HEADS))
    sin = jnp.tile(jnp.repeat(jnp.sin(theta), 2, axis=1), (1, RET_HEADS))
    log_gamma = jnp.log(1.0 - jnp.exp2(-5.0 - jnp.arange(RET_HEADS, dtype=F32)))
    lg = log_gamma.reshape(2, 4)
    i = jnp.arange(CHUNK, dtype=F32)
    diff = i[:, None] - i[None, :]
    low = diff >= 0
    blocks = jnp.exp(jnp.where(low[None, None], lg[:, :, None, None] * diff[None, None], -jnp.inf))
    eye4 = jnp.eye(4, dtype=F32)
    dmat = jnp.einsum('ghij,hk->ghikj', blocks, eye4).reshape(2, 4 * CHUNK, 4 * CHUNK)
    qdec = jnp.exp(lg[:, :, None] * (i[None, None, :] + 1.0)).reshape(2, 4 * CHUNK, 1)
    qdec = jnp.broadcast_to(qdec, (2, 4 * CHUNK, LANE))
    ktail = jnp.exp(lg[:, :, None] * (CHUNK - 1.0 - i[None, None, :])).reshape(2, 4 * CHUNK, 1)
    ktail = jnp.broadcast_to(ktail, (2, 4 * CHUNK, LANE))
    gam_rows = jnp.repeat(jnp.exp(lg * CHUNK), 32, axis=1)
    gam = jnp.broadcast_to(gam_rows[:, :, None], (2, LANE, 256))
    return cos, sin, dmat, qdec, ktail, gam


def _ret_mixer(p, tables, norm_w):
    b, s, n = p.shape
    cos, sin, dmat, qdec, ktail, gam = tables
    full = lambda *shape: pl.BlockSpec(shape, lambda bi, i: (0,) * len(shape))
    return pl.pallas_call(
        _ret_kernel,
        grid=(b, s // TILE),
        in_specs=[pl.BlockSpec((None, TILE, n), lambda bi, i: (bi, i, 0)),
                  pl.BlockSpec((TILE, 256), lambda bi, i: (i, 0)),
                  pl.BlockSpec((TILE, 256), lambda bi, i: (i, 0)),
                  full(2, 4 * CHUNK, 4 * CHUNK), full(2, 4 * CHUNK, LANE), full(2, 4 * CHUNK, LANE),
                  full(2, LANE, 256), full(1, WIDTH)],
        out_specs=pl.BlockSpec((None, TILE, WIDTH), lambda bi, i: (bi, i, 0)),
        out_shape=jax.ShapeDtypeStruct((b, s, WIDTH), BF16),
        scratch_shapes=[pltpu.VMEM((2, LANE, 256), F32)],
        name="ret_mixer",
        compiler_params=pltpu.CompilerParams(dimension_semantics=("arbitrary", "arbitrary"),
                                             vmem_limit_bytes=VMEM_LIMIT),
    )(p, cos, sin, dmat, qdec, ktail, gam, norm_w)


def _ssd_kernel(p_ref, cw_ref, cb_ref, dtb_ref, a_ref, dskip_ref, nw_ref, e64_ref, e128_ref,
                o_ref, hist_ref, state_ref):
    first = pl.program_id(1) == 0

    @pl.when(first)
    def _():
        state_ref[...] = jnp.zeros(state_ref.shape, F32)

    xbc = _silu(_causal_conv(hist_ref, first, p_ref[:, 0:1024], cw_ref) + cb_ref[...])
    z = p_ref[:, 1024:1536]
    dt = _softplus(p_ref[:, 1536:1664] + dtb_ref[...])
    log_a = dt * a_ref[...]
    tile = xbc.shape[0]
    xs = xbc[:, 0:512]
    g_cum = _select_rows(_tri_chunk(tile, SSD_CHUNK), log_a)
    g_all = _select_cols(g_cum, e128_ref[...])
    val = xs * _select_cols(dt, e64_ref[...])
    ri = _iota((SSD_CHUNK, SSD_CHUNK), 0)
    ci = _iota((SSD_CHUNK, SSD_CHUNK), 1)
    low = ri >= ci
    hmask = [_lane_mask(256, 64 * h, 64 * h + 64) for h in range(4)]
    n_chunks = tile // SSD_CHUNK
    items = [(g, c) for g in range(2) for c in range(n_chunks)]
    heads = range(4)

    def rows_of(t, c):
        return t[c * SSD_CHUNK:(c + 1) * SSD_CHUNK]

    bm = {(g, c): rows_of(xbc[:, 512 + LANE * g:512 + LANE * (g + 1)], c) for g, c in items}
    cm = {(g, c): rows_of(xbc[:, 768 + LANE * g:768 + LANE * (g + 1)], c) for g, c in items}
    vc = {(g, c): rows_of(val[:, 256 * g:256 * (g + 1)], c).astype(BF16) for g, c in items}
    gc = {(g, c, h): rows_of(g_all[:, LANE * (4 * g + h):LANE * (4 * g + h + 1)], c)
          for g, c in items for h in heads}
    g_last = {key: t[SSD_CHUNK - 1:SSD_CHUNK, :] for key, t in gc.items()}
    qk = {it: _dot_nt(cm[it], bm[it]) for it in items}
    s_cat = {it: jnp.concatenate(
        [qk[it] * jnp.exp(jnp.where(low, gc[(*it, h)] - gc[(*it, h)].T, -jnp.inf)) for h in heads],
        axis=-1).astype(BF16) for it in items}
    q_cat = {it: jnp.concatenate([cm[it] * jnp.exp(gc[(*it, h)]) for h in heads], axis=-1).astype(BF16)
             for it in items}
    v_st = {it: jnp.concatenate([vc[it] * m.astype(BF16) for m in hmask], axis=0) for it in items}
    y_in = {it: _dot(s_cat[it], v_st[it]) for it in items}
    k_tail = {key: (bm[key[0:2]] * jnp.exp(g_last[key] - t)).astype(BF16) for key, t in gc.items()}
    upd = {key: _dot_tn(k_tail[key], vc[key[0:2]]) * hmask[key[2]] for key in gc}
    y_groups = []
    for g in range(2):
        state = [state_ref[g, SSD_STATE * h:SSD_STATE * (h + 1), :] for h in heads]
        ys = []
        for c in range(n_chunks):
            ys.append(y_in[g, c] + _dot(q_cat[g, c], jnp.concatenate(state, axis=0)))
            state = [state[h] * jnp.concatenate([jnp.exp(g_last[g, c, h])] * 2, axis=-1) + upd[g, c, h]
                     for h in heads]
        for h in heads:
            state_ref[g, SSD_STATE * h:SSD_STATE * (h + 1), :] = state[h]
        y_groups.append(jnp.concatenate(ys, axis=0))
    y = jnp.concatenate(y_groups, axis=-1) + xs * dskip_ref[...]
    y = y * _silu(z)
    outs = []
    for g in range(2):
        yg = y[:, 256 * g:256 * (g + 1)]
        ms = jnp.sum(yg * yg, axis=-1, keepdims=True) * (1.0 / 256.0)
        outs.append(yg * lax.rsqrt(ms + NORM_EPS))
    o_ref[...] = (jnp.concatenate(outs, axis=-1) * nw_ref[...]).astype(o_ref.dtype)


def _ssd_mixer(p, conv_w, conv_b, dt_bias, a_vec, d_skip, norm_w, e64, e128):
    b, s, n = p.shape
    full = lambda *shape: pl.BlockSpec(shape, lambda bi, i: (0,) * len(shape))
    return pl.pallas_call(
        _ssd_kernel,
        grid=(b, s // TILE),
        in_specs=[pl.BlockSpec((None, TILE, n), lambda bi, i: (bi, i, 0)),
                  full(CONV_K, 1024), full(1, 1024), full(1, LANE), full(1, LANE),
                  full(1, WIDTH), full(1, WIDTH), full(LANE, WIDTH), full(LANE, SSD_HEADS * LANE)],
        out_specs=pl.BlockSpec((None, TILE, WIDTH), lambda bi, i: (bi, i, 0)),
        out_shape=jax.ShapeDtypeStruct((b, s, WIDTH), BF16),
        scratch_shapes=[pltpu.VMEM((SUBLANE + TILE, 1024), F32),
                        pltpu.VMEM((2, 4 * SSD_STATE, 256), F32)],
        name="ssd_mixer",
        compiler_params=pltpu.CompilerParams(dimension_semantics=("arbitrary", "arbitrary"),
                                             vmem_limit_bytes=VMEM_LIMIT),
    )(p, conv_w, conv_b, dt_bias, a_vec, d_skip, norm_w, e64, e128)


def _gdn_kernel(p_ref, cw_ref, dtb_ref, a_ref, nw_ref, eb_ref, eg_ref,
                o_ref, hist_ref, state_ref):
    first = pl.program_id(1) == 0

    @pl.when(first)
    def _():
        state_ref[...] = jnp.zeros(state_ref.shape, F32)

    qkv = _silu(_causal_conv(hist_ref, first, p_ref[:, 0:1536], cw_ref))
    z = p_ref[:, 1536:2048]
    ba = p_ref[:, 2048:2176]
    beta_blk = _sigmoid(ba)
    log_a = a_ref[...] * _softplus(ba + dtb_ref[...])
    tile = qkv.shape[0]
    g_all = _select_cols(_select_rows(_tri_chunk(tile, CHUNK), log_a), eg_ref[...])
    beta_all = _select_cols(beta_blk, eb_ref[...])

    def l2n(t):
        return t * lax.rsqrt(jnp.sum(t * t, axis=-1, keepdims=True) + 1e-6)

    two = 2 * CHUNK
    ri = _iota((two, two), 0)
    ci = _iota((two, two), 1)
    same = jnp.right_shift(ri, 6) == jnp.right_shift(ci, 6)
    incl = same & (ri >= ci)
    strict = same & (ri > ci)
    n_chunks = tile // CHUNK
    items = [(c, pr) for c in range(n_chunks) for pr in range(2)]

    qh = [l2n(qkv[:, LANE * h:LANE * (h + 1)]) * (GDN_HEAD ** -0.5) for h in range(GDN_HEADS)]
    kh = [l2n(qkv[:, 512 + LANE * h:512 + LANE * (h + 1)]) for h in range(GDN_HEADS)]
    vh = [qkv[:, 1024 + LANE * h:1024 + LANE * (h + 1)] for h in range(GDN_HEADS)]

    def rows_of(ts, it):
        c, pr = it
        return jnp.concatenate([ts[h][c * CHUNK:(c + 1) * CHUNK] for h in (2 * pr, 2 * pr + 1)], axis=0)

    g_heads = [g_all[:, LANE * h:LANE * (h + 1)] for h in range(GDN_HEADS)]
    b_heads = [beta_all[:, LANE * h:LANE * (h + 1)] for h in range(GDN_HEADS)]
    gc = {it: rows_of(g_heads, it) for it in items}
    bt = {it: rows_of(b_heads, it) for it in items}
    q_r = {it: rows_of(qh, it) for it in items}
    k_r = {it: rows_of(kh, it) for it in items}
    v_r = {it: rows_of(vh, it) for it in items}
    kb_r = {it: k_r[it] * bt[it] for it in items}
    eg = {it: jnp.exp(gc[it]) for it in items}
    dec_i = {it: jnp.exp(jnp.where(incl, gc[it] - gc[it].T, -jnp.inf)) for it in items}
    g_last = {it: jnp.concatenate(
        [jnp.broadcast_to(gc[it][CHUNK - 1:CHUNK, :], (CHUNK, LANE)),
         jnp.broadcast_to(gc[it][two - 1:two, :], (CHUNK, LANE))], axis=0) for it in items}
    k_tail = {it: (k_r[it] * jnp.exp(g_last[it] - gc[it])).astype(BF16) for it in items}
    lam = {it: (jnp.exp(gc[it][CHUNK - 1:CHUNK, :]), jnp.exp(gc[it][two - 1:two, :])) for it in items}
    rhs = {it: jnp.concatenate([v_r[it] * bt[it], kb_r[it] * eg[it]], axis=-1).astype(BF16)
           for it in items}
    q_dec = {it: q_r[it] * eg[it] for it in items}
    sc = {it: _dot_nt(jnp.concatenate([kb_r[it], q_r[it]], axis=0), k_r[it]) for it in items}
    ms = [-jnp.where(strict, sc[it][0:two] * dec_i[it], 0.0) for it in items]
    attn = {it: (sc[it][two:2 * two] * dec_i[it]).astype(BF16) for it in items}
    t_invs = dict(zip(items, _neumann_inverses(ms)))
    uw = {it: _dot(t_invs[it], rhs[it]).astype(BF16) for it in items}
    au = {it: _dot(attn[it], uw[it]) for it in items}
    o_0 = {it: au[it][:, 0:LANE] for it in items}
    o_q = {it: (q_dec[it] - au[it][:, LANE:2 * LANE]).astype(BF16) for it in items}
    p_t, q_t = {}, {}
    for c, pr in items:
        for j in range(2):
            hs = slice(j * CHUNK, (j + 1) * CHUNK)
            pq = _dot_tn(jnp.concatenate([-uw[c, pr][hs, LANE:2 * LANE], uw[c, pr][hs, 0:LANE]], axis=-1),
                         k_tail[c, pr][hs])
            p_t[c, 2 * pr + j] = pq[0:LANE].astype(BF16)
            q_t[c, 2 * pr + j] = pq[LANE:2 * LANE]

    states = [state_ref[h] for h in range(GDN_HEADS)]
    os = [[] for _ in range(GDN_HEADS)]
    for c in range(n_chunks):
        for h in range(GDN_HEADS):
            pr, j = divmod(h, 2)
            hs = slice(j * CHUNK, (j + 1) * CHUNK)
            os[h].append(_dot_nt(o_q[c, pr][hs], states[h]) + o_0[c, pr][hs])
            states[h] = states[h] * lam[c, pr][j] + _dot(states[h], p_t[c, h]) + q_t[c, h]
    outs = []
    for h in range(GDN_HEADS):
        state_ref[h] = states[h]
        oh = jnp.concatenate(os[h], axis=0)
        ms_h = jnp.sum(oh * oh, axis=-1, keepdims=True) * (1.0 / GDN_HEAD)
        outs.append(oh * lax.rsqrt(ms_h + NORM_EPS) * nw_ref[...])
    o_ref[...] = (jnp.concatenate(outs, axis=-1) * _silu(z)).astype(o_ref.dtype)


def _gdn_mixer(p, conv_w, dt_bias, a_vec, norm_w, e_beta, e_g):
    b, s, n = p.shape
    full = lambda *shape: pl.BlockSpec(shape, lambda bi, i: (0,) * len(shape))
    return pl.pallas_call(
        _gdn_kernel,
        grid=(b, s // TILE),
        in_specs=[pl.BlockSpec((None, TILE, n), lambda bi, i: (bi, i, 0)),
                  full(CONV_K, 1536), full(1, LANE), full(1, LANE), full(1, GDN_HEAD),
                  full(LANE, GDN_HEADS * LANE), full(LANE, GDN_HEADS * LANE)],
        out_specs=pl.BlockSpec((None, TILE, WIDTH), lambda bi, i: (bi, i, 0)),
        out_shape=jax.ShapeDtypeStruct((b, s, WIDTH), BF16),
        scratch_shapes=[pltpu.VMEM((SUBLANE + TILE, 1536), F32),
                        pltpu.VMEM((GDN_HEADS, GDN_HEAD, GDN_HEAD), F32)],
        name="gdn_mixer",
        compiler_params=pltpu.CompilerParams(dimension_semantics=("arbitrary", "arbitrary"),
                                             vmem_limit_bytes=VMEM_LIMIT),
    )(p, conv_w, dt_bias, a_vec, norm_w, e_beta, e_g)


def _rwkv_kernel(p_ref, *refs):
    params, (o_ref, hist_ref, state_ref) = refs[:-3], refs[-3:]
    for j in range(p_ref.shape[0]):
        _rwkv_row(p_ref.at[j], *params, o_ref.at[j], hist_ref.at[j], state_ref.at[j])


def _rwkv_row(p_ref, mu_ref, wup_ref, w0_ref, aup_ref, a0_ref, kk_ref, ka_ref,
              rk_ref, lnw_ref, lnb_ref, o_ref, hist_ref, state_ref):
    first = pl.program_id(1) == 0

    @pl.when(first)
    def _():
        state_ref[...] = jnp.zeros(state_ref.shape, F32)

    cur = p_ref[:, 0:1664]
    (prev,) = _shifted(hist_ref, first, cur, 1)
    mixed = cur + (prev - cur) * mu_ref[...]
    r = mixed[:, 0:512]
    k = mixed[:, 512:1024]
    v = mixed[:, 1024:1536]
    lo = mixed[:, 1536:1664]
    z = p_ref[:, 1664:2176]
    tile = r.shape[0]
    w_log = -_softplus(-(w0_ref[...] + _dot(jnp.tanh(lo), wup_ref[...]))) - 0.5
    logw = -jnp.exp(w_log)
    iclr = _sigmoid(a0_ref[...] + _dot(lo, aup_ref[...]))
    kkr = k * kk_ref[...]
    kk = kkr * lax.rsqrt(_seg64_sum(kkr * kkr) + 1e-6)
    k = k * (1.0 + (iclr - 1.0) * ka_ref[...])
    a = -kk
    b = kk * iclr
    g_inc = _select_rows(_tri_chunk(tile, CHUNK), logw)
    g_exc = g_inc - logw
    e_inc = jnp.exp(g_inc)
    e_neg = jnp.exp(-g_inc)
    r_t = r * e_inc
    a_t = a * jnp.exp(g_exc)
    k_h = k * e_neg
    b_h = b * e_neg

    two = 2 * CHUNK
    ri = _iota((two, two), 0)
    ci = _iota((two, two), 1)
    same = jnp.right_shift(ri, 6) == jnp.right_shift(ci, 6)
    incl = (same & (ri >= ci)).astype(F32)
    strict = (same & (ri > ci)).astype(F32)
    m0 = _lane_mask(LANE, 0, 64).astype(BF16)
    m1 = _lane_mask(LANE, 64, LANE).astype(BF16)

    def stack(t):
        return jnp.concatenate([t * m0, t * m1], axis=0)

    n_chunks = tile // CHUNK
    n_pairs = WIDTH // LANE
    items = [(c, pr) for c in range(n_chunks) for pr in range(n_pairs)]

    def blk(t, it):
        c, pr = it
        return t[c * CHUNK:(c + 1) * CHUNK, LANE * pr:LANE * (pr + 1)]

    a_b, r_b, kh_b, bh_b, v_b = (t.astype(BF16) for t in (a_t, r_t, k_h, b_h, v))
    g_last = {it: g_inc[(it[0] + 1) * CHUNK - 1:(it[0] + 1) * CHUNK, LANE * it[1]:LANE * (it[1] + 1)]
              for it in items}
    tail = {it: jnp.exp(g_last[it] - blk(g_inc, it)) for it in items}
    e_last = {it: jnp.exp(g_last[it]) for it in items}
    ar_st = {it: jnp.concatenate([stack(blk(a_b, it)), stack(blk(r_b, it))], axis=0) for it in items}
    kb_st = {it: jnp.concatenate([stack(blk(kh_b, it)), stack(blk(bh_b, it))], axis=0) for it in items}
    v_st = {it: stack(blk(v_b, it)) for it in items}
    kbt_st = {it: jnp.concatenate([stack((blk(k, it) * tail[it]).astype(BF16)),
                                   stack((blk(b, it) * tail[it]).astype(BF16))], axis=0) for it in items}
    sc = {it: _dot_nt(ar_st[it], kb_st[it]) for it in items}
    a_abs = [sc[it][0:two, two:2 * two] * strict for it in items]
    a_rb = {it: (sc[it][two:2 * two, two:2 * two] * incl).astype(BF16) for it in items}
    a_akrk = {it: jnp.concatenate([sc[it][0:two, 0:two] * strict, sc[it][two:2 * two, 0:two] * incl],
                                  axis=0).astype(BF16) for it in items}
    av = {it: _dot(a_akrk[it], v_st[it]) for it in items}
    t_invs = dict(zip(items, _neumann_inverses(a_abs)))
    wu = {it: _dot(t_invs[it], jnp.concatenate([ar_st[it][0:two], av[it][0:two].astype(BF16)], axis=-1))
          .astype(BF16) for it in items}
    ry = {it: jnp.concatenate([ar_st[it][two:2 * two].astype(F32), av[it][two:2 * two]], axis=-1)
          + _dot(a_rb[it], wu[it]) for it in items}
    r_q = {it: ry[it][:, 0:LANE].astype(BF16) for it in items}
    y_0 = {it: ry[it][:, LANE:2 * LANE] for it in items}
    p_l = {it: _dot_tn(wu[it][:, 0:LANE], kbt_st[it][two:2 * two]).astype(BF16) for it in items}
    q_l = {it: _dot_tn(jnp.concatenate([v_st[it], wu[it][:, LANE:2 * LANE]], axis=0), kbt_st[it])
           for it in items}


    states = [state_ref[pr] for pr in range(n_pairs)]
    ys = [[] for _ in range(n_pairs)]
    for c in range(n_chunks):
        for pr in range(n_pairs):
            y_st = _dot_nt(r_q[c, pr], states[pr]) + y_0[c, pr]
            states[pr] = states[pr] * e_last[c, pr] + _dot(states[pr], p_l[c, pr]) + q_l[c, pr]
            ys[pr].append(y_st[0:CHUNK] + y_st[CHUNK:two])
    for pr in range(n_pairs):
        state_ref[pr] = states[pr]
    y = jnp.concatenate([jnp.concatenate(ys[pr], axis=0) for pr in range(n_pairs)], axis=-1)
    mu = _seg64_sum(y) * (1.0 / RW_HEAD)
    yc = y - mu
    var = _seg64_sum(yc * yc) * (1.0 / RW_HEAD)
    yn = yc * lax.rsqrt(var + RW_GN_EPS) * lnw_ref[...] + lnb_ref[...]
    bonus = _seg64_sum(r * k * rk_ref[...]) * v
    o_ref[...] = ((yn + bonus) * _silu(z)).astype(o_ref.dtype)


def _rwkv_mixer(p, mu, w_up, w0, a_up, a0, k_k, k_a, r_k, ln_w, ln_b):
    b, s, n = p.shape
    full = lambda *shape: pl.BlockSpec(shape, lambda bi, i: (0,) * len(shape))
    return pl.pallas_call(
        _rwkv_kernel,
        grid=(b // ROWS, s // TILE),
        in_specs=[pl.BlockSpec((ROWS, TILE, n), lambda bi, i: (bi, i, 0)),
                  full(1, 1664)] + [full(LANE, WIDTH), full(1, WIDTH)] * 2
                 + [full(1, WIDTH)] * 5,
        out_specs=pl.BlockSpec((ROWS, TILE, WIDTH), lambda bi, i: (bi, i, 0)),
        out_shape=jax.ShapeDtypeStruct((b, s, WIDTH), BF16),
        scratch_shapes=[pltpu.VMEM((ROWS, SUBLANE + TILE, 1664), F32),
                        pltpu.VMEM((ROWS, 4, LANE, LANE), F32)],
        name="rwkv_mixer",
        compiler_params=pltpu.CompilerParams(dimension_semantics=("arbitrary", "arbitrary"),
                                             vmem_limit_bytes=VMEM_LIMIT),
    )(p, mu, w_up, w0, a_up, a0, k_k, k_a, r_k, ln_w, ln_b)


def _merge_kernel(x_ref, nw_ref, wg_ref, ua_ref, ub_ref, uc_ref, ud_ref, wb_ref, wo_ref, fw_ref,
                  o_ref, *, final_norm):
    x = x_ref[...]
    h = (x * lax.rsqrt(jnp.mean(x * x, axis=-1, keepdims=True) + NORM_EPS) * nw_ref[...]).astype(BF16)
    merged = None
    for i, u_ref in enumerate((ua_ref, ub_ref, uc_ref, ud_ref)):
        gate = _sigmoid(jnp.dot(h, wg_ref[:, D_MODEL * i:D_MODEL * (i + 1)],
                                preferred_element_type=F32))
        term = gate * jnp.dot(u_ref[...], wb_ref[i], preferred_element_type=F32)
        merged = term if merged is None else merged + term
    out = x + jnp.dot(merged.astype(BF16), wo_ref[...], preferred_element_type=F32)
    if final_norm:
        out = out * lax.rsqrt(jnp.mean(out * out, axis=-1, keepdims=True) + NORM_EPS) * fw_ref[...]
    o_ref[...] = out


def _merge(x2, norm_w, w_gate, us, w_branch, w_out, final_w, final_norm):
    tokens, d = x2.shape
    row = lambda n: pl.BlockSpec((PROJ_TILE, n), lambda i: (i, 0))
    const = lambda *shape: pl.BlockSpec(shape, lambda i: (0,) * len(shape))
    return pl.pallas_call(
        functools.partial(_merge_kernel, final_norm=final_norm),
        grid=(tokens // PROJ_TILE,),
        in_specs=[row(d), const(1, d), const(d, 4 * d)] + [row(WIDTH)] * 4
                 + [const(4, WIDTH, d), const(d, d), const(1, d)],
        out_specs=row(d),
        out_shape=jax.ShapeDtypeStruct((tokens, d), F32),
        name="merge",
        compiler_params=pltpu.CompilerParams(dimension_semantics=("arbitrary",),
                                             vmem_limit_bytes=VMEM_LIMIT),
    )(x2, norm_w, w_gate, *us, w_branch, w_out, final_w)


def _pad_cols(w, n):
    return jnp.pad(w, ((0, 0), (0, n - w.shape[1])))


def _split_w_in(w_in):
    widths = (WIDTH, WIDTH, WIDTH, RW_LORA, RW_LORA, WIDTH,
              RET_QK, RET_QK, WIDTH, WIDTH,
              WIDTH + 4 * SSD_STATE, WIDTH, SSD_HEADS,
              3 * WIDTH, WIDTH, GDN_HEADS, GDN_HEADS,
              4 * D_MODEL)
    offs = np.cumsum((0,) + widths)
    seg = [w_in[:, int(offs[i]):int(offs[i + 1])] for i in range(len(widths))]
    (rw_r, rw_k, rw_v, rw_wlo, rw_alo, rw_z, rt_q, rt_k, rt_v, rt_z,
     sd_xbc, sd_z, sd_dt, gd_qkv, gd_z, gd_b, gd_a, gates) = seg

    def swap_pairs(w):
        w2 = w.reshape(w.shape[0], -1, 2)
        return jnp.stack([-w2[..., 1], w2[..., 0]], axis=-1).reshape(w.shape)

    w_rw = jnp.concatenate([rw_r, rw_k, rw_v, rw_wlo, rw_alo, rw_z], axis=1)
    w_rt = jnp.concatenate([rt_q, rt_k, swap_pairs(rt_q), swap_pairs(rt_k), rt_v, rt_z], axis=1)
    w_sd = jnp.concatenate([sd_xbc, sd_z, _pad_cols(sd_dt, LANE)], axis=1)
    w_gd = jnp.concatenate([gd_qkv, gd_z, _pad_cols(jnp.concatenate([gd_b, gd_a], axis=1), LANE)],
                           axis=1)
    return [w.astype(BF16) for w in (w_rw, w_rt, w_sd, w_gd, gates)]


def _head_expand(n_heads, width, offset=0):
    e = np.zeros((LANE, n_heads * width), np.float32)
    for h in range(n_heads):
        e[offset + h, h * width:(h + 1) * width] = 1.0
    return jnp.asarray(e)


def _row(v, n=None):
    v = v.reshape(1, -1).astype(F32)
    return v if n is None else _pad_cols(v, n)


def kernel(x, norm_w, w_in, rwkv_mu_rkv, rwkv_mu_wa, rwkv_w_up, rwkv_w0, rwkv_a_up, rwkv_a0,
           rwkv_k_k, rwkv_k_a, rwkv_r_k, rwkv_ln_w, rwkv_ln_b, ret_norm_w, ssd_conv_w, ssd_conv_b,
           ssd_dt_bias, ssd_A_log, ssd_D, ssd_norm_w, gdn_conv_w, gdn_dt_bias, gdn_A_log, gdn_norm_w,
           w_branch, w_out, final_norm_w):
    b, s, d = x.shape
    depth = norm_w.shape[0]
    tokens = b * s
    ret_tables = _ret_tables(s)
    e64 = _head_expand(SSD_HEADS, 64)
    e128_ssd = _head_expand(SSD_HEADS, LANE)
    e_beta = _head_expand(GDN_HEADS, LANE, 0)
    e_g = _head_expand(GDN_HEADS, LANE, GDN_HEADS)
    zeros_lora = jnp.zeros((RW_LORA, WIDTH), F32)
    x2 = x.reshape(tokens, d)
    for l in range(depth):
        w_rw, w_rt, w_sd, w_gd, w_gate = _split_w_in(w_in[l])
        nw = _row(norm_w[l])
        p_rw = _project(x2, nw, w_rw).reshape(b, s, -1)
        p_rt = _project(x2, nw, w_rt).reshape(b, s, -1)
        p_sd = _project(x2, nw, w_sd).reshape(b, s, -1)
        p_gd = _project(x2, nw, w_gd).reshape(b, s, -1)

        mu = jnp.concatenate([rwkv_mu_rkv[l].reshape(1, -1), rwkv_mu_wa[l].reshape(1, -1)], axis=1)
        w_up = jnp.concatenate([rwkv_w_up[l], zeros_lora], axis=0).astype(BF16)
        a_up = jnp.concatenate([zeros_lora, rwkv_a_up[l]], axis=0).astype(BF16)
        u_a = _rwkv_mixer(p_rw, mu, w_up, _row(rwkv_w0[l]), a_up, _row(rwkv_a0[l]),
                          _row(rwkv_k_k[l]), _row(rwkv_k_a[l]), _row(rwkv_r_k[l]),
                          _row(rwkv_ln_w[l]), _row(rwkv_ln_b[l]))
        u_b = _ret_mixer(p_rt, ret_tables, _row(ret_norm_w[l]))
        u_c = _ssd_mixer(p_sd, ssd_conv_w[l], _row(ssd_conv_b[l]), _row(ssd_dt_bias[l], LANE),
                         _row(-jnp.exp(ssd_A_log[l].astype(F32)), LANE),
                         _row(jnp.repeat(ssd_D[l], 64)), _row(ssd_norm_w[l]), e64, e128_ssd)
        gdn_bias = jnp.concatenate([jnp.zeros((GDN_HEADS,), F32), gdn_dt_bias[l]])
        gdn_a = jnp.concatenate([jnp.zeros((GDN_HEADS,), F32), -jnp.exp(gdn_A_log[l].astype(F32))])
        u_d = _gdn_mixer(p_gd, gdn_conv_w[l], _row(gdn_bias, LANE), _row(gdn_a, LANE),
                         _row(gdn_norm_w[l]), e_beta, e_g)
        us = [u.reshape(tokens, WIDTH) for u in (u_a, u_b, u_c, u_d)]
        x2 = _merge(x2, nw, w_gate, us, w_branch[l].astype(BF16), w_out[l].astype(BF16),
                    _row(final_norm_w), final_norm=(l == depth - 1))
    return x2.reshape(b, s, d)
```

```python
import functools
import math

import numpy as np
import jax
import jax.numpy as jnp
from jax import lax
from jax.experimental import pallas as pl
from jax.experimental.pallas import tpu as pltpu

F32 = jnp.float32
BF16 = jnp.bfloat16

D_MODEL = 1024
WIDTH = 512
NORM_EPS = 1e-6
CONV_K = 4
RW_HEAD = 64
RW_LORA = 64
RW_GN_EPS = 64e-5
RET_HEADS = 8
RET_QK = 256
ROPE_BASE = 10000.0
SSD_HEADS = 8
SSD_STATE = 128
GDN_HEADS = 4
GDN_HEAD = 128

LANE = 128
SUBLANE = 8
CHUNK = 64
SSD_CHUNK = 128
TILE = 512
ROWS = 1
PROJ_TILE = 512
VMEM_LIMIT = 56 * 1024 * 1024


def _dot(a, b):
    return jnp.dot(a.astype(BF16), b.astype(BF16), preferred_element_type=F32)


def _dot_nt(a, b):
    return lax.dot_general(a.astype(BF16), b.astype(BF16), (((1,), (1,)), ((), ())),
                           preferred_element_type=F32)


def _dot_tn(a, b):
    return lax.dot_general(a.astype(BF16), b.astype(BF16), (((0,), (0,)), ((), ())),
                           preferred_element_type=F32)


def _bf16_pieces(x):
    hi = x.astype(BF16)
    r1 = x - hi.astype(F32)
    mid = r1.astype(BF16)
    lo = (r1 - mid.astype(F32)).astype(BF16)
    return hi, mid, lo


def _select_rows(sel, x):
    n = x.shape[1]
    y = jnp.dot(sel.astype(BF16), jnp.concatenate(_bf16_pieces(x), axis=-1),
                preferred_element_type=F32)
    return y[:, 0:n] + y[:, n:2 * n] + y[:, 2 * n:3 * n]


def _select_cols(x, sel):
    m = x.shape[0]
    y = jnp.dot(jnp.concatenate(_bf16_pieces(x), axis=0), sel.astype(BF16),
                preferred_element_type=F32)
    return y[0:m] + y[m:2 * m] + y[2 * m:3 * m]


def _sigmoid(x):
    return 1.0 / (1.0 + jnp.exp(-x))


def _silu(x):
    return x * _sigmoid(x)


def _softplus(x):
    return jnp.maximum(x, 0.0) + jnp.log(1.0 + jnp.exp(-jnp.abs(x)))


def _iota(shape, dim):
    return lax.broadcasted_iota(jnp.int32, shape, dim)


def _lane_mask(width, lo, hi):
    lane = _iota((1, width), 1)
    return ((lane >= lo) & (lane < hi)).astype(F32)


def _chunk_cumsum(x, chunk):
    group = 2 * LANE
    r = _iota((group, group), 0)
    c = _iota((group, group), 1)
    sh = int(math.log2(chunk))
    tri = ((r >= c) & (jnp.right_shift(r, sh) == jnp.right_shift(c, sh))).astype(F32)
    parts = [_select_rows(tri, x[i:i + group]) for i in range(0, x.shape[0], group)]
    return parts[0] if len(parts) == 1 else jnp.concatenate(parts, axis=0)


def _seg64_sum(x):
    outs = []
    for j in range(x.shape[1] // LANE):
        xb = x[:, LANE * j:LANE * (j + 1)]
        lo = _iota(xb.shape, 1) < 64
        s_lo = jnp.sum(jnp.where(lo, xb, 0.0), axis=-1, keepdims=True)
        s_hi = jnp.sum(jnp.where(lo, 0.0, xb), axis=-1, keepdims=True)
        outs.append(jnp.where(lo, s_lo, s_hi))
    return outs[0] if len(outs) == 1 else jnp.concatenate(outs, axis=-1)


def _shifted(hist_ref, first, cur, n_shift):
    tile = cur.shape[0]

    @pl.when(first)
    def _():
        hist_ref[0:SUBLANE, :] = jnp.zeros((SUBLANE, cur.shape[1]), F32)

    hist_ref[SUBLANE:SUBLANE + tile, :] = cur
    outs = [hist_ref[SUBLANE - j:SUBLANE - j + tile, :] for j in range(1, n_shift + 1)]
    hist_ref[0:SUBLANE, :] = cur[tile - SUBLANE:tile, :]
    return outs


def _causal_conv(hist_ref, first, cur, w_ref):
    d1, d2, d3 = _shifted(hist_ref, first, cur, CONV_K - 1)
    return (cur * w_ref[3:4, :] + d1 * w_ref[2:3, :] + d2 * w_ref[1:2, :] + d3 * w_ref[0:1, :])


def _neumann_inverses(ns):
    size = ns[0].shape[0]
    assert size == LANE and len(ns) % 2 == 0
    eye = (_iota((size, size), 0) == _iota((size, size), 1)).astype(F32)
    zero = jnp.zeros((size, size), BF16)

    def weights(row):
        row = row.astype(BF16)
        return jnp.concatenate([jnp.concatenate([row[:, 0:size], zero], axis=-1),
                                jnp.concatenate([zero, row[:, size:2 * size]], axis=-1)], axis=0)

    ns = [jnp.concatenate([ns[i], ns[i + 1]], axis=-1) for i in range(0, len(ns), 2)]
    eye2 = jnp.concatenate([eye, eye], axis=-1)
    ps = [eye2 + n for n in ns]
    ns = [_dot(n, weights(n)) for n in ns]
    levels = int(math.log2(CHUNK)) - 1
    for _ in range(levels - 1):
        boths = [_dot(jnp.concatenate([p, n], axis=0), weights(n)) for p, n in zip(ps, ns)]
        ps = [p + both[:size] for p, both in zip(ps, boths)]
        ns = [both[size:] for both in boths]
    ps = [p + _dot(p, weights(n)) for p, n in zip(ps, ns)]
    return [half for p in ps for half in (p[:, 0:size], p[:, size:2 * size])]


def _proj_kernel(x_ref, nw_ref, w_ref, o_ref):
    x = x_ref[...]
    h = x * lax.rsqrt(jnp.mean(x * x, axis=-1, keepdims=True) + NORM_EPS) * nw_ref[...]
    o_ref[...] = jnp.dot(h.astype(BF16), w_ref[...], preferred_element_type=F32)


def _project(x2, norm_w, w):
    tokens, d = x2.shape
    n = w.shape[1]
    return pl.pallas_call(
        _proj_kernel,
        grid=(tokens // PROJ_TILE,),
        in_specs=[pl.BlockSpec((PROJ_TILE, d), lambda i: (i, 0)),
                  pl.BlockSpec((1, d), lambda i: (0, 0)),
                  pl.BlockSpec((d, n), lambda i: (0, 0))],
        out_specs=pl.BlockSpec((PROJ_TILE, n), lambda i: (i, 0)),
        out_shape=jax.ShapeDtypeStruct((tokens, n), F32),
        name="in_proj",
        compiler_params=pltpu.CompilerParams(dimension_semantics=("arbitrary",),
                                             vmem_limit_bytes=VMEM_LIMIT),
    )(x2, norm_w, w)


def _ret_kernel(p_ref, cos_ref, sin_ref, dmat_ref, qdec_ref, ktail_ref, gam_ref, nw_ref,
                o_ref, state_ref):
    first = pl.program_id(1) == 0

    @pl.when(first)
    def _():
        state_ref[...] = jnp.zeros(state_ref.shape, F32)

    def rotary(t):
        blocks = []
        for j in range(t.shape[1] // LANE):
            lanes = slice(LANE * j, LANE * (j + 1))
            tb = t[:, lanes]
            blocks.append(tb * cos_ref[:, lanes]
                          + pltpu.roll(tb, LANE - 1, axis=1) * sin_ref[0, :, lanes]
                          + pltpu.roll(tb, 1, axis=1) * sin_ref[1, :, lanes])
        return jnp.concatenate(blocks, axis=-1)

    q = rotary(p_ref[:, 0:256])
    k = rotary(p_ref[:, 256:512]) * (32.0 ** -0.5)
    v = p_ref[:, 512:1024]
    z = p_ref[:, 1024:1536]
    tile = q.shape[0]
    qmasks = [_lane_mask(LANE, 32 * h, 32 * h + 32) for h in range(4)]
    vmasks = [_lane_mask(256, 64 * h, 64 * h + 64) for h in range(4)]
    n_chunks = tile // CHUNK
    items = [(g, c) for g in range(2) for c in range(n_chunks)]

    def stacked(t, it, width, masks):
        g, c = it
        blk = t[c * CHUNK:(c + 1) * CHUNK, width * g:width * (g + 1)]
        return jnp.concatenate([blk * m for m in masks], axis=0)

    q_st = {it: stacked(q, it, LANE, qmasks) for it in items}
    k_st = {it: stacked(k, it, LANE, qmasks) for it in items}
    v_st = {it: stacked(v, it, 256, vmasks).astype(BF16) for it in items}
    q_dec = {it: (q_st[it] * qdec_ref[it[0]]).astype(BF16) for it in items}
    k_tl = {it: (k_st[it] * ktail_ref[it[0]]).astype(BF16) for it in items}
    s = {it: (_dot_nt(q_st[it], k_st[it]) * dmat_ref[it[0]]).astype(BF16) for it in items}
    y_in = {it: _dot(s[it], v_st[it]) for it in items}
    upd = {it: _dot_tn(k_tl[it], v_st[it]) for it in items}
    y_groups = []
    for g in range(2):
        state = state_ref[g]
        ys = []
        for c in range(n_chunks):
            y_st = y_in[g, c] + _dot(q_dec[g, c], state)
            state = gam_ref[g] * state + upd[g, c]
            ys.append(y_st[0:CHUNK] + y_st[CHUNK:2 * CHUNK]
                      + y_st[2 * CHUNK:3 * CHUNK] + y_st[3 * CHUNK:4 * CHUNK])
        state_ref[g] = state
        y_groups.append(jnp.concatenate(ys, axis=0))
    y = jnp.concatenate(y_groups, axis=-1)
    ms = _seg64_sum(y * y) * (1.0 / 64.0)
    y = y * lax.rsqrt(ms + NORM_EPS) * nw_ref[...]
    o_ref[...] = (y * _silu(z)).astype(o_ref.dtype)


def _ret_tables(seq):
    half = 16
    angle = 1.0 / (ROPE_BASE ** jnp.linspace(0.0, 1.0, half, dtype=F32))
    theta = jnp.arange(seq, dtype=F32)[:, None] * angle[None, :]
    cos = jnp.tile(jnp.repeat(jnp.cos(theta), 2, axis=1), (1, RET_HEADS))
    sin = jnp.tile(jnp.repeat(jnp.sin(theta), 2, axis=1), (1, RET_HEADS))
    even = (jnp.arange(RET_QK) % 2 == 0)[None, :]
    sin = jnp.stack([jnp.where(even, -sin, 0.0), jnp.where(even, 0.0, sin)])
    log_gamma = jnp.log(1.0 - jnp.exp2(-5.0 - jnp.arange(RET_HEADS, dtype=F32)))
    lg = log_gamma.reshape(2, 4)
    i = jnp.arange(CHUNK, dtype=F32)
    diff = i[:, None] - i[None, :]
    low = diff >= 0
    blocks = jnp.exp(jnp.where(low[None, None], lg[:, :, None, None] * diff[None, None], -jnp.inf))
    eye4 = jnp.eye(4, dtype=F32)
    dmat = jnp.einsum('ghij,hk->ghikj', blocks, eye4).reshape(2, 4 * CHUNK, 4 * CHUNK)
    qdec = jnp.exp(lg[:, :, None] * (i[None, None, :] + 1.0)).reshape(2, 4 * CHUNK, 1)
    qdec = jnp.broadcast_to(qdec, (2, 4 * CHUNK, LANE))
    ktail = jnp.exp(lg[:, :, None] * (CHUNK - 1.0 - i[None, None, :])).reshape(2, 4 * CHUNK, 1)
    ktail = jnp.broadcast_to(ktail, (2, 4 * CHUNK, LANE))
    gam_rows = jnp.repeat(jnp.exp(lg * CHUNK), 32, axis=1)
    gam = jnp.broadcast_to(gam_rows[:, :, None], (2, LANE, 256))
    return cos, sin, dmat, qdec, ktail, gam


def _ret_mixer(p, tables, norm_w):
    b, s, n = p.shape
    cos, sin, dmat, qdec, ktail, gam = tables
    full = lambda *shape: pl.BlockSpec(shape, lambda bi, i: (0,) * len(shape))
    return pl.pallas_call(
        _ret_kernel,
        grid=(b, s // TILE),
        in_specs=[pl.BlockSpec((None, TILE, n), lambda bi, i: (bi, i, 0)),
                  pl.BlockSpec((TILE, RET_QK), lambda bi, i: (i, 0)),
                  pl.BlockSpec((2, TILE, RET_QK), lambda bi, i: (0, i, 0)),
                  full(2, 4 * CHUNK, 4 * CHUNK), full(2, 4 * CHUNK, LANE), full(2, 4 * CHUNK, LANE),
                  full(2, LANE, 256), full(1, WIDTH)],
        out_specs=pl.BlockSpec((None, TILE, WIDTH), lambda bi, i: (bi, i, 0)),
        out_shape=jax.ShapeDtypeStruct((b, s, WIDTH), BF16),
        scratch_shapes=[pltpu.VMEM((2, LANE, 256), F32)],
        name="ret_mixer",
        compiler_params=pltpu.CompilerParams(dimension_semantics=("arbitrary", "arbitrary"),
                                             vmem_limit_bytes=VMEM_LIMIT),
    )(p, cos, sin, dmat, qdec, ktail, gam, norm_w)


def _ssd_kernel(p_ref, cw_ref, cb_ref, dtb_ref, a_ref, dskip_ref, nw_ref, e64_ref, e128_ref,
                o_ref, hist_ref, state_ref):
    first = pl.program_id(1) == 0

    @pl.when(first)
    def _():
        state_ref[...] = jnp.zeros(state_ref.shape, F32)

    xbc = _silu(_causal_conv(hist_ref, first, p_ref[:, 0:1024], cw_ref) + cb_ref[...])
    z = p_ref[:, 1024:1536]
    dt = _softplus(p_ref[:, 1536:1664] + dtb_ref[...])
    log_a = dt * a_ref[...]
    tile = xbc.shape[0]
    xs = xbc[:, 0:512]
    g_cum = _chunk_cumsum(log_a, SSD_CHUNK)
    g_all = _select_cols(g_cum, e128_ref[...])
    val = xs * _select_cols(dt, e64_ref[...])
    ri = _iota((SSD_CHUNK, SSD_CHUNK), 0)
    ci = _iota((SSD_CHUNK, SSD_CHUNK), 1)
    low = ri >= ci
    hmask = [_lane_mask(256, 64 * h, 64 * h + 64) for h in range(4)]
    n_chunks = tile // SSD_CHUNK
    items = [(g, c) for g in range(2) for c in range(n_chunks)]
    heads = range(4)

    def rows_of(t, c):
        return t[c * SSD_CHUNK:(c + 1) * SSD_CHUNK]

    bm = {(g, c): rows_of(xbc[:, 512 + LANE * g:512 + LANE * (g + 1)], c) for g, c in items}
    cm = {(g, c): rows_of(xbc[:, 768 + LANE * g:768 + LANE * (g + 1)], c) for g, c in items}
    vc = {(g, c): rows_of(val[:, 256 * g:256 * (g + 1)], c).astype(BF16) for g, c in items}
    gc = {(g, c, h): rows_of(g_all[:, LANE * (4 * g + h):LANE * (4 * g + h + 1)], c)
          for g, c in items for h in heads}
    g_last = {key: t[SSD_CHUNK - 1:SSD_CHUNK, :] for key, t in gc.items()}
    qk = {it: _dot_nt(cm[it], bm[it]) for it in items}
    s_cat = {it: jnp.concatenate(
        [qk[it] * jnp.exp(jnp.where(low, gc[(*it, h)] - gc[(*it, h)].T, -jnp.inf)) for h in heads],
        axis=-1).astype(BF16) for it in items}
    q_cat = {it: jnp.concatenate([cm[it] * jnp.exp(gc[(*it, h)]) for h in heads], axis=-1).astype(BF16)
             for it in items}
    v_st = {it: jnp.concatenate([vc[it] * m.astype(BF16) for m in hmask], axis=0) for it in items}
    y_in = {it: _dot(s_cat[it], v_st[it]) for it in items}
    k_tail = {key: (bm[key[0:2]] * jnp.exp(g_last[key] - t)).astype(BF16) for key, t in gc.items()}
    upd = {key: _dot_tn(k_tail[key], vc[key[0:2]]) * hmask[key[2]] for key in gc}
    y_groups = []
    for g in range(2):
        state = [state_ref[g, SSD_STATE * h:SSD_STATE * (h + 1), :] for h in heads]
        ys = []
        for c in range(n_chunks):
            ys.append(y_in[g, c] + _dot(q_cat[g, c], jnp.concatenate(state, axis=0)))
            state = [state[h] * jnp.concatenate([jnp.exp(g_last[g, c, h])] * 2, axis=-1) + upd[g, c, h]
                     for h in heads]
        for h in heads:
            state_ref[g, SSD_STATE * h:SSD_STATE * (h + 1), :] = state[h]
        y_groups.append(jnp.concatenate(ys, axis=0))
    y = jnp.concatenate(y_groups, axis=-1) + xs * dskip_ref[...]
    y = y * _silu(z)
    outs = []
    for g in range(2):
        yg = y[:, 256 * g:256 * (g + 1)]
        ms = jnp.sum(yg * yg, axis=-1, keepdims=True) * (1.0 / 256.0)
        outs.append(yg * lax.rsqrt(ms + NORM_EPS))
    o_ref[...] = (jnp.concatenate(outs, axis=-1) * nw_ref[...]).astype(o_ref.dtype)


def _ssd_mixer(p, conv_w, conv_b, dt_bias, a_vec, d_skip, norm_w, e64, e128):
    b, s, n = p.shape
    full = lambda *shape: pl.BlockSpec(shape, lambda bi, i: (0,) * len(shape))
    return pl.pallas_call(
        _ssd_kernel,
        grid=(b, s // TILE),
        in_specs=[pl.BlockSpec((None, TILE, n), lambda bi, i: (bi, i, 0)),
                  full(CONV_K, 1024), full(1, 1024), full(1, LANE), full(1, LANE),
                  full(1, WIDTH), full(1, WIDTH), full(LANE, WIDTH), full(LANE, SSD_HEADS * LANE)],
        out_specs=pl.BlockSpec((None, TILE, WIDTH), lambda bi, i: (bi, i, 0)),
        out_shape=jax.ShapeDtypeStruct((b, s, WIDTH), BF16),
        scratch_shapes=[pltpu.VMEM((SUBLANE + TILE, 1024), F32),
                        pltpu.VMEM((2, 4 * SSD_STATE, 256), F32)],
        name="ssd_mixer",
        compiler_params=pltpu.CompilerParams(dimension_semantics=("arbitrary", "arbitrary"),
                                             vmem_limit_bytes=VMEM_LIMIT),
    )(p, conv_w, conv_b, dt_bias, a_vec, d_skip, norm_w, e64, e128)


def _gdn_kernel(p_ref, cw_ref, dtb_ref, a_ref, nw_ref, eb_ref, eg_ref,
                o_ref, hist_ref, state_ref):
    first = pl.program_id(1) == 0

    @pl.when(first)
    def _():
        state_ref[...] = jnp.zeros(state_ref.shape, F32)

    qkv = _silu(_causal_conv(hist_ref, first, p_ref[:, 0:1536], cw_ref))
    z = p_ref[:, 1536:2048]
    ba = p_ref[:, 2048:2176]
    beta_blk = _sigmoid(ba)
    log_a = a_ref[...] * _softplus(ba + dtb_ref[...])
    tile = qkv.shape[0]
    g_all = _select_cols(_chunk_cumsum(log_a, CHUNK), eg_ref[...])
    beta_all = _select_cols(beta_blk, eb_ref[...])

    def l2n(t):
        return t * lax.rsqrt(jnp.sum(t * t, axis=-1, keepdims=True) + 1e-6)

    two = 2 * CHUNK
    ri = _iota((two, two), 0)
    ci = _iota((two, two), 1)
    same = jnp.right_shift(ri, 6) == jnp.right_shift(ci, 6)
    incl = same & (ri >= ci)
    strict = same & (ri > ci)
    n_chunks = tile // CHUNK
    items = [(c, pr) for c in range(n_chunks) for pr in range(2)]

    qh = [l2n(qkv[:, LANE * h:LANE * (h + 1)]) * (GDN_HEAD ** -0.5) for h in range(GDN_HEADS)]
    kh = [l2n(qkv[:, 512 + LANE * h:512 + LANE * (h + 1)]) for h in range(GDN_HEADS)]
    vh = [qkv[:, 1024 + LANE * h:1024 + LANE * (h + 1)] for h in range(GDN_HEADS)]

    def rows_of(ts, it):
        c, pr = it
        return jnp.concatenate([ts[h][c * CHUNK:(c + 1) * CHUNK] for h in (2 * pr, 2 * pr + 1)], axis=0)

    g_heads = [g_all[:, LANE * h:LANE * (h + 1)] for h in range(GDN_HEADS)]
    b_heads = [beta_all[:, LANE * h:LANE * (h + 1)] for h in range(GDN_HEADS)]
    gc = {it: rows_of(g_heads, it) for it in items}
    bt = {it: rows_of(b_heads, it) for it in items}
    q_r = {it: rows_of(qh, it) for it in items}
    k_r = {it: rows_of(kh, it) for it in items}
    v_r = {it: rows_of(vh, it) for it in items}
    kb_r = {it: k_r[it] * bt[it] for it in items}
    eg = {it: jnp.exp(gc[it]) for it in items}
    dec_i = {it: jnp.exp(jnp.where(incl, gc[it] - gc[it].T, -jnp.inf)) for it in items}
    g_last = {it: jnp.concatenate(
        [jnp.broadcast_to(gc[it][CHUNK - 1:CHUNK, :], (CHUNK, LANE)),
         jnp.broadcast_to(gc[it][two - 1:two, :], (CHUNK, LANE))], axis=0) for it in items}
    k_tail = {it: (k_r[it] * jnp.exp(g_last[it] - gc[it])).astype(BF16) for it in items}
    lam = {it: (jnp.exp(gc[it][CHUNK - 1:CHUNK, :]), jnp.exp(gc[it][two - 1:two, :])) for it in items}
    rhs = {it: jnp.concatenate([v_r[it] * bt[it], kb_r[it] * eg[it]], axis=-1).astype(BF16)
           for it in items}
    q_dec = {it: q_r[it] * eg[it] for it in items}
    sc = {it: _dot_nt(jnp.concatenate([kb_r[it], q_r[it]], axis=0), k_r[it]) for it in items}
    ms = [-jnp.where(strict, sc[it][0:two] * dec_i[it], 0.0) for it in items]
    attn = {it: (sc[it][two:2 * two] * dec_i[it]).astype(BF16) for it in items}
    t_invs = dict(zip(items, _neumann_inverses(ms)))
    uw = {it: _dot(t_invs[it], rhs[it]).astype(BF16) for it in items}
    au = {it: _dot(attn[it], uw[it]) for it in items}
    o_0 = {it: au[it][:, 0:LANE] for it in items}
    o_q = {it: (q_dec[it] - au[it][:, LANE:2 * LANE]).astype(BF16) for it in items}
    p_t, q_t = {}, {}
    for c, pr in items:
        for j in range(2):
            hs = slice(j * CHUNK, (j + 1) * CHUNK)
            pq = _dot_tn(jnp.concatenate([-uw[c, pr][hs, LANE:2 * LANE], uw[c, pr][hs, 0:LANE]], axis=-1),
                         k_tail[c, pr][hs])
            p_t[c, 2 * pr + j] = pq[0:LANE].astype(BF16)
            q_t[c, 2 * pr + j] = pq[LANE:2 * LANE]

    states = [state_ref[h] for h in range(GDN_HEADS)]
    os = [[] for _ in range(GDN_HEADS)]
    for c in range(n_chunks):
        for h in range(GDN_HEADS):
            pr, j = divmod(h, 2)
            hs = slice(j * CHUNK, (j + 1) * CHUNK)
            os[h].append(_dot_nt(o_q[c, pr][hs], states[h]) + o_0[c, pr][hs])
            states[h] = states[h] * lam[c, pr][j] + _dot(states[h], p_t[c, h]) + q_t[c, h]
    outs = []
    for h in range(GDN_HEADS):
        state_ref[h] = states[h]
        oh = jnp.concatenate(os[h], axis=0)
        ms_h = jnp.sum(oh * oh, axis=-1, keepdims=True) * (1.0 / GDN_HEAD)
        outs.append(oh * lax.rsqrt(ms_h + NORM_EPS) * nw_ref[...])
    o_ref[...] = (jnp.concatenate(outs, axis=-1) * _silu(z)).astype(o_ref.dtype)


def _gdn_mixer(p, conv_w, dt_bias, a_vec, norm_w, e_beta, e_g):
    b, s, n = p.shape
    full = lambda *shape: pl.BlockSpec(shape, lambda bi, i: (0,) * len(shape))
    return pl.pallas_call(
        _gdn_kernel,
        grid=(b, s // TILE),
        in_specs=[pl.BlockSpec((None, TILE, n), lambda bi, i: (bi, i, 0)),
                  full(CONV_K, 1536), full(1, LANE), full(1, LANE), full(1, GDN_HEAD),
                  full(LANE, GDN_HEADS * LANE), full(LANE, GDN_HEADS * LANE)],
        out_specs=pl.BlockSpec((None, TILE, WIDTH), lambda bi, i: (bi, i, 0)),
        out_shape=jax.ShapeDtypeStruct((b, s, WIDTH), BF16),
        scratch_shapes=[pltpu.VMEM((SUBLANE + TILE, 1536), F32),
                        pltpu.VMEM((GDN_HEADS, GDN_HEAD, GDN_HEAD), F32)],
        name="gdn_mixer",
        compiler_params=pltpu.CompilerParams(dimension_semantics=("arbitrary", "arbitrary"),
                                             vmem_limit_bytes=VMEM_LIMIT),
    )(p, conv_w, dt_bias, a_vec, norm_w, e_beta, e_g)


def _rwkv_kernel(p_ref, *refs):
    params, (o_ref, hist_ref, state_ref) = refs[:-3], refs[-3:]
    for j in range(p_ref.shape[0]):
        _rwkv_row(p_ref.at[j], *params, o_ref.at[j], hist_ref.at[j], state_ref.at[j])


def _rwkv_row(p_ref, mu_ref, wup_ref, w0_ref, aup_ref, a0_ref, kk_ref, ka_ref,
              rk_ref, lnw_ref, lnb_ref, o_ref, hist_ref, state_ref):
    first = pl.program_id(1) == 0

    @pl.when(first)
    def _():
        state_ref[...] = jnp.zeros(state_ref.shape, F32)

    cur = p_ref[:, 0:1664]
    (prev,) = _shifted(hist_ref, first, cur, 1)
    mixed = cur + (prev - cur) * mu_ref[...]
    r = mixed[:, 0:512]
    k = mixed[:, 512:1024]
    v = mixed[:, 1024:1536]
    lo = mixed[:, 1536:1664]
    z = p_ref[:, 1664:2176]
    tile = r.shape[0]
    logw = -math.exp(-0.5) * _sigmoid(w0_ref[...] + _dot(jnp.tanh(lo), wup_ref[...]))
    iclr = _sigmoid(a0_ref[...] + _dot(lo, aup_ref[...]))
    kkr = k * kk_ref[...]
    kk = kkr * lax.rsqrt(_seg64_sum(kkr * kkr) + 1e-6)
    k = k * (1.0 + (iclr - 1.0) * ka_ref[...])
    a = -kk
    b = kk * iclr
    g_inc = _chunk_cumsum(logw, CHUNK)
    g_exc = g_inc - logw
    e_inc = jnp.exp(g_inc)
    e_neg = jnp.exp(-g_inc)
    r_t = r * e_inc
    a_t = a * jnp.exp(g_exc)
    k_h = k * e_neg
    b_h = b * e_neg

    two = 2 * CHUNK
    ri = _iota((two, two), 0)
    ci = _iota((two, two), 1)
    same = jnp.right_shift(ri, 6) == jnp.right_shift(ci, 6)
    incl = (same & (ri >= ci)).astype(F32)
    strict = (same & (ri > ci)).astype(F32)
    m0 = _lane_mask(LANE, 0, 64).astype(BF16)
    m1 = _lane_mask(LANE, 64, LANE).astype(BF16)

    def stack(t):
        return jnp.concatenate([t * m0, t * m1], axis=0)

    n_chunks = tile // CHUNK
    n_pairs = WIDTH // LANE
    items = [(c, pr) for c in range(n_chunks) for pr in range(n_pairs)]

    def blk(t, it):
        c, pr = it
        return t[c * CHUNK:(c + 1) * CHUNK, LANE * pr:LANE * (pr + 1)]

    a_b, r_b, kh_b, bh_b, v_b = (t.astype(BF16) for t in (a_t, r_t, k_h, b_h, v))
    g_last = {it: g_inc[(it[0] + 1) * CHUNK - 1:(it[0] + 1) * CHUNK, LANE * it[1]:LANE * (it[1] + 1)]
              for it in items}
    tail = {it: jnp.exp(g_last[it] - blk(g_inc, it)) for it in items}
    e_last = {it: jnp.exp(g_last[it]) for it in items}
    ar_st = {it: jnp.concatenate([stack(blk(a_b, it)), stack(blk(r_b, it))], axis=0) for it in items}
    kb_st = {it: jnp.concatenate([stack(blk(kh_b, it)), stack(blk(bh_b, it))], axis=0) for it in items}
    v_st = {it: stack(blk(v_b, it)) for it in items}
    kbt_st = {it: jnp.concatenate([stack((blk(k, it) * tail[it]).astype(BF16)),
                                   stack((blk(b, it) * tail[it]).astype(BF16))], axis=0) for it in items}
    sc = {it: _dot_nt(ar_st[it], kb_st[it]) for it in items}
    a_abs = [sc[it][0:two, two:2 * two] * strict for it in items]
    a_rb = {it: (sc[it][two:2 * two, two:2 * two] * incl).astype(BF16) for it in items}
    a_akrk = {it: jnp.concatenate([sc[it][0:two, 0:two] * strict, sc[it][two:2 * two, 0:two] * incl],
                                  axis=0).astype(BF16) for it in items}
    av = {it: _dot(a_akrk[it], v_st[it]) for it in items}
    t_invs = dict(zip(items, _neumann_inverses(a_abs)))
    wu = {it: _dot(t_invs[it], jnp.concatenate([ar_st[it][0:two], av[it][0:two].astype(BF16)], axis=-1))
          .astype(BF16) for it in items}
    ry = {it: jnp.concatenate([ar_st[it][two:2 * two].astype(F32), av[it][two:2 * two]], axis=-1)
          + _dot(a_rb[it], wu[it]) for it in items}
    r_q = {it: ry[it][:, 0:LANE].astype(BF16) for it in items}
    y_0 = {it: ry[it][:, LANE:2 * LANE] for it in items}
    p_l = {it: _dot_tn(wu[it][:, 0:LANE], kbt_st[it][two:2 * two]).astype(BF16) for it in items}
    q_l = {it: _dot_tn(jnp.concatenate([v_st[it], wu[it][:, LANE:2 * LANE]], axis=0), kbt_st[it])
           for it in items}


    states = [state_ref[pr] for pr in range(n_pairs)]
    ys = [[] for _ in range(n_pairs)]
    for c in range(n_chunks):
        for pr in range(n_pairs):
            y_st = _dot_nt(r_q[c, pr], states[pr]) + y_0[c, pr]
            states[pr] = states[pr] * e_last[c, pr] + _dot(states[pr], p_l[c, pr]) + q_l[c, pr]
            ys[pr].append(y_st[0:CHUNK] + y_st[CHUNK:two])
    for pr in range(n_pairs):
        state_ref[pr] = states[pr]
    y = jnp.concatenate([jnp.concatenate(ys[pr], axis=0) for pr in range(n_pairs)], axis=-1)
    mu = _seg64_sum(y) * (1.0 / RW_HEAD)
    yc = y - mu
    var = _seg64_sum(yc * yc) * (1.0 / RW_HEAD)
    yn = yc * lax.rsqrt(var + RW_GN_EPS) * lnw_ref[...] + lnb_ref[...]
    bonus = _seg64_sum(r * k * rk_ref[...]) * v
    o_ref[...] = ((yn + bonus) * _silu(z)).astype(o_ref.dtype)


def _rwkv_mixer(p, mu, w_up, w0, a_up, a0, k_k, k_a, r_k, ln_w, ln_b):
    b, s, n = p.shape
    full = lambda *shape: pl.BlockSpec(shape, lambda bi, i: (0,) * len(shape))
    return pl.pallas_call(
        _rwkv_kernel,
        grid=(b // ROWS, s // TILE),
        in_specs=[pl.BlockSpec((ROWS, TILE, n), lambda bi, i: (bi, i, 0)),
                  full(1, 1664)] + [full(LANE, WIDTH), full(1, WIDTH)] * 2
                 + [full(1, WIDTH)] * 5,
        out_specs=pl.BlockSpec((ROWS, TILE, WIDTH), lambda bi, i: (bi, i, 0)),
        out_shape=jax.ShapeDtypeStruct((b, s, WIDTH), BF16),
        scratch_shapes=[pltpu.VMEM((ROWS, SUBLANE + TILE, 1664), F32),
                        pltpu.VMEM((ROWS, 4, LANE, LANE), F32)],
        name="rwkv_mixer",
        compiler_params=pltpu.CompilerParams(dimension_semantics=("arbitrary", "arbitrary"),
                                             vmem_limit_bytes=VMEM_LIMIT),
    )(p, mu, w_up, w0, a_up, a0, k_k, k_a, r_k, ln_w, ln_b)


def _merge_kernel(x_ref, nw_ref, wg_ref, ua_ref, ub_ref, uc_ref, ud_ref, wb_ref, wo_ref, fw_ref,
                  o_ref, *, final_norm):
    x = x_ref[...]
    h = (x * lax.rsqrt(jnp.mean(x * x, axis=-1, keepdims=True) + NORM_EPS) * nw_ref[...]).astype(BF16)
    merged = None
    for i, u_ref in enumerate((ua_ref, ub_ref, uc_ref, ud_ref)):
        gate = _sigmoid(jnp.dot(h, wg_ref[:, D_MODEL * i:D_MODEL * (i + 1)],
                                preferred_element_type=F32))
        term = gate * jnp.dot(u_ref[...], wb_ref[i], preferred_element_type=F32)
        merged = term if merged is None else merged + term
    out = x + jnp.dot(merged.astype(BF16), wo_ref[...], preferred_element_type=F32)
    if final_norm:
        out = out * lax.rsqrt(jnp.mean(out * out, axis=-1, keepdims=True) + NORM_EPS) * fw_ref[...]
    o_ref[...] = out


def _merge(x2, norm_w, w_gate, us, w_branch, w_out, final_w, final_norm):
    tokens, d = x2.shape
    row = lambda n: pl.BlockSpec((PROJ_TILE, n), lambda i: (i, 0))
    const = lambda *shape: pl.BlockSpec(shape, lambda i: (0,) * len(shape))
    return pl.pallas_call(
        functools.partial(_merge_kernel, final_norm=final_norm),
        grid=(tokens // PROJ_TILE,),
        in_specs=[row(d), const(1, d), const(d, 4 * d)] + [row(WIDTH)] * 4
                 + [const(4, WIDTH, d), const(d, d), const(1, d)],
        out_specs=row(d),
        out_shape=jax.ShapeDtypeStruct((tokens, d), F32),
        name="merge",
        compiler_params=pltpu.CompilerParams(dimension_semantics=("arbitrary",),
                                             vmem_limit_bytes=VMEM_LIMIT),
    )(x2, norm_w, w_gate, *us, w_branch, w_out, final_w)


def _pad_cols(w, n):
    return jnp.pad(w, ((0, 0), (0, n - w.shape[1])))


def _split_w_in(w_in):
    widths = (WIDTH, WIDTH, WIDTH, RW_LORA, RW_LORA, WIDTH,
              RET_QK, RET_QK, WIDTH, WIDTH,
              WIDTH + 4 * SSD_STATE, WIDTH, SSD_HEADS,
              3 * WIDTH, WIDTH, GDN_HEADS, GDN_HEADS,
              4 * D_MODEL)
    offs = np.cumsum((0,) + widths)
    seg = [w_in[:, int(offs[i]):int(offs[i + 1])] for i in range(len(widths))]
    (rw_r, rw_k, rw_v, rw_wlo, rw_alo, rw_z, rt_q, rt_k, rt_v, rt_z,
     sd_xbc, sd_z, sd_dt, gd_qkv, gd_z, gd_b, gd_a, gates) = seg

    w_rw = jnp.concatenate([rw_r, rw_k, rw_v, rw_wlo, rw_alo, rw_z], axis=1)
    w_rt = jnp.concatenate([rt_q, rt_k, rt_v, rt_z], axis=1)
    w_sd = jnp.concatenate([sd_xbc, sd_z, _pad_cols(sd_dt, LANE)], axis=1)
    w_gd = jnp.concatenate([gd_qkv, gd_z, _pad_cols(jnp.concatenate([gd_b, gd_a], axis=1), LANE)],
                           axis=1)
    return [w.astype(BF16) for w in (w_rw, w_rt, w_sd, w_gd, gates)]


def _head_expand(n_heads, width, offset=0):
    e = np.zeros((LANE, n_heads * width), np.float32)
    for h in range(n_heads):
        e[offset + h, h * width:(h + 1) * width] = 1.0
    return jnp.asarray(e)


def _row(v, n=None):
    v = v.reshape(1, -1).astype(F32)
    return v if n is None else _pad_cols(v, n)


def kernel(x, norm_w, w_in, rwkv_mu_rkv, rwkv_mu_wa, rwkv_w_up, rwkv_w0, rwkv_a_up, rwkv_a0,
           rwkv_k_k, rwkv_k_a, rwkv_r_k, rwkv_ln_w, rwkv_ln_b, ret_norm_w, ssd_conv_w, ssd_conv_b,
           ssd_dt_bias, ssd_A_log, ssd_D, ssd_norm_w, gdn_conv_w, gdn_dt_bias, gdn_A_log, gdn_norm_w,
           w_branch, w_out, final_norm_w):
    b, s, d = x.shape
    depth = norm_w.shape[0]
    tokens = b * s
    ret_tables = _ret_tables(s)
    e64 = _head_expand(SSD_HEADS, 64)
    e128_ssd = _head_expand(SSD_HEADS, LANE)
    e_beta = _head_expand(GDN_HEADS, LANE, 0)
    e_g = _head_expand(GDN_HEADS, LANE, GDN_HEADS)
    zeros_lora = jnp.zeros((RW_LORA, WIDTH), F32)
    x2 = x.reshape(tokens, d)
    for l in range(depth):
        w_rw, w_rt, w_sd, w_gd, w_gate = _split_w_in(w_in[l])
        nw = _row(norm_w[l])
        p_rw = _project(x2, nw, w_rw).reshape(b, s, -1)
        p_rt = _project(x2, nw, w_rt).reshape(b, s, -1)
        p_sd = _project(x2, nw, w_sd).reshape(b, s, -1)
        p_gd = _project(x2, nw, w_gd).reshape(b, s, -1)

        mu = jnp.concatenate([rwkv_mu_rkv[l].reshape(1, -1), rwkv_mu_wa[l].reshape(1, -1)], axis=1)
        w_up = jnp.concatenate([rwkv_w_up[l], zeros_lora], axis=0).astype(BF16)
        a_up = jnp.concatenate([zeros_lora, rwkv_a_up[l]], axis=0).astype(BF16)
        u_a = _rwkv_mixer(p_rw, mu, w_up, _row(rwkv_w0[l]), a_up, _row(rwkv_a0[l]),
                          _row(rwkv_k_k[l]), _row(rwkv_k_a[l]), _row(rwkv_r_k[l]),
                          _row(rwkv_ln_w[l]), _row(rwkv_ln_b[l]))
        u_b = _ret_mixer(p_rt, ret_tables, _row(ret_norm_w[l]))
        u_c = _ssd_mixer(p_sd, ssd_conv_w[l], _row(ssd_conv_b[l]), _row(ssd_dt_bias[l], LANE),
                         _row(-jnp.exp(ssd_A_log[l].astype(F32)), LANE),
                         _row(jnp.repeat(ssd_D[l], 64)), _row(ssd_norm_w[l]), e64, e128_ssd)
        gdn_bias = jnp.concatenate([jnp.zeros((GDN_HEADS,), F32), gdn_dt_bias[l]])
        gdn_a = jnp.concatenate([jnp.zeros((GDN_HEADS,), F32), -jnp.exp(gdn_A_log[l].astype(F32))])
        u_d = _gdn_mixer(p_gd, gdn_conv_w[l], _row(gdn_bias, LANE), _row(gdn_a, LANE),
                         _row(gdn_norm_w[l]), e_beta, e_g)
        us = [u.reshape(tokens, WIDTH) for u in (u_a, u_b, u_c, u_d)]
        x2 = _merge(x2, nw, w_gate, us, w_branch[l].astype(BF16), w_out[l].astype(BF16),
                    _row(final_norm_w), final_norm=(l == depth - 1))
    return x2.reshape(b, s, d)
```

```python
import functools
import math

import numpy as np
import jax
import jax.numpy as jnp
from jax import lax
from jax.experimental import pallas as pl
from jax.experimental.pallas import tpu as pltpu

F32 = jnp.float32
BF16 = jnp.bfloat16

D_MODEL = 1024
WIDTH = 512
NORM_EPS = 1e-6
CONV_K = 4
RW_HEAD = 64
RW_LORA = 64
RW_GN_EPS = 64e-5
RET_HEADS = 8
RET_QK = 256
ROPE_BASE = 10000.0
SSD_HEADS = 8
SSD_STATE = 128
GDN_HEADS = 4
GDN_HEAD = 128

LANE = 128
SUBLANE = 8
CHUNK = 64
SSD_CHUNK = 128
TILE = 512
LIGHT_TILE = 512
RW_TILE = 512
ROWS = 1
PROJ_TILE = 512
VMEM_LIMIT = 56 * 1024 * 1024


def _dot(a, b):
    return jnp.dot(a.astype(BF16), b.astype(BF16), preferred_element_type=F32)


def _dot_nt(a, b):
    return lax.dot_general(a.astype(BF16), b.astype(BF16), (((1,), (1,)), ((), ())),
                           preferred_element_type=F32)


def _dot_tn(a, b):
    return lax.dot_general(a.astype(BF16), b.astype(BF16), (((0,), (0,)), ((), ())),
                           preferred_element_type=F32)


def _bf16_pieces(x):
    hi = x.astype(BF16)
    r1 = x - hi.astype(F32)
    mid = r1.astype(BF16)
    lo = (r1 - mid.astype(F32)).astype(BF16)
    return hi, mid, lo


def _select_rows(sel, x):
    n = x.shape[1]
    y = jnp.dot(sel.astype(BF16), jnp.concatenate(_bf16_pieces(x), axis=-1),
                preferred_element_type=F32)
    return y[:, 0:n] + y[:, n:2 * n] + y[:, 2 * n:3 * n]


def _select_cols(x, sel):
    m = x.shape[0]
    y = jnp.dot(jnp.concatenate(_bf16_pieces(x), axis=0), sel.astype(BF16),
                preferred_element_type=F32)
    return y[0:m] + y[m:2 * m] + y[2 * m:3 * m]


def _sigmoid(x):
    return 1.0 / (1.0 + jnp.exp(-x))


def _silu(x):
    return x * _sigmoid(x)


def _softplus(x):
    return jnp.maximum(x, 0.0) + jnp.log(1.0 + jnp.exp(-jnp.abs(x)))


def _iota(shape, dim):
    return lax.broadcasted_iota(jnp.int32, shape, dim)


def _lane_mask(width, lo, hi):
    lane = _iota((1, width), 1)
    return ((lane >= lo) & (lane < hi)).astype(F32)


def _chunk_cumsum(x, chunk):
    group = 2 * LANE
    r = _iota((group, group), 0)
    c = _iota((group, group), 1)
    sh = int(math.log2(chunk))
    tri = ((r >= c) & (jnp.right_shift(r, sh) == jnp.right_shift(c, sh))).astype(F32)
    parts = [_select_rows(tri, x[i:i + group]) for i in range(0, x.shape[0], group)]
    return parts[0] if len(parts) == 1 else jnp.concatenate(parts, axis=0)


def _seg64_sum(x):
    outs = []
    for j in range(x.shape[1] // LANE):
        xb = x[:, LANE * j:LANE * (j + 1)]
        lo = _iota(xb.shape, 1) < 64
        s_lo = jnp.sum(jnp.where(lo, xb, 0.0), axis=-1, keepdims=True)
        s_hi = jnp.sum(jnp.where(lo, 0.0, xb), axis=-1, keepdims=True)
        outs.append(jnp.where(lo, s_lo, s_hi))
    return outs[0] if len(outs) == 1 else jnp.concatenate(outs, axis=-1)


def _shifted(hist_ref, first, cur, n_shift):
    tile = cur.shape[0]

    @pl.when(first)
    def _():
        hist_ref[0:SUBLANE, :] = jnp.zeros((SUBLANE, cur.shape[1]), F32)

    hist_ref[SUBLANE:SUBLANE + tile, :] = cur
    outs = [hist_ref[SUBLANE - j:SUBLANE - j + tile, :] for j in range(1, n_shift + 1)]
    hist_ref[0:SUBLANE, :] = cur[tile - SUBLANE:tile, :]
    return outs


def _causal_conv(hist_ref, first, cur, w_ref):
    d1, d2, d3 = _shifted(hist_ref, first, cur, CONV_K - 1)
    return (cur * w_ref[3:4, :] + d1 * w_ref[2:3, :] + d2 * w_ref[1:2, :] + d3 * w_ref[0:1, :])


def _interleave(programs, lead):
    live = list(programs)
    for _ in range(lead):
        next(live[0], None)
    while live:
        for prog in list(live):
            try:
                next(prog)
            except StopIteration:
                live.remove(prog)


def _neumann_inverses(ns):
    steps = _neumann_inverse_steps(ns)
    while True:
        try:
            next(steps)
        except StopIteration as done:
            return done.value


def _neumann_inverse_steps(ns):
    size = ns[0].shape[0]
    assert size == LANE and len(ns) % 2 == 0
    eye = (_iota((size, size), 0) == _iota((size, size), 1)).astype(F32)
    zero = jnp.zeros((size, size), BF16)

    def weights(row):
        row = row.astype(BF16)
        return jnp.concatenate([jnp.concatenate([row[:, 0:size], zero], axis=-1),
                                jnp.concatenate([zero, row[:, size:2 * size]], axis=-1)], axis=0)

    ns = [jnp.concatenate([ns[i], ns[i + 1]], axis=-1) for i in range(0, len(ns), 2)]
    eye2 = jnp.concatenate([eye, eye], axis=-1)
    ps = [eye2 + n for n in ns]
    ns = [_dot(n, weights(n)) for n in ns]
    yield
    levels = int(math.log2(CHUNK)) - 1
    for _ in range(levels - 1):
        boths = [_dot(jnp.concatenate([p, n], axis=0), weights(n)) for p, n in zip(ps, ns)]
        ps = [p + both[:size] for p, both in zip(ps, boths)]
        ns = [both[size:] for both in boths]
        yield
    ps = [p + _dot(p, weights(n)) for p, n in zip(ps, ns)]
    return [half for p in ps for half in (p[:, 0:size], p[:, size:2 * size])]


def _proj_kernel(x_ref, nw_ref, w_ref, o_ref):
    x = x_ref[...]
    h = x * lax.rsqrt(jnp.mean(x * x, axis=-1, keepdims=True) + NORM_EPS) * nw_ref[...]
    o_ref[...] = jnp.dot(h.astype(BF16), w_ref[...], preferred_element_type=F32)


def _project(x2, norm_w, w):
    tokens, d = x2.shape
    n = w.shape[1]
    return pl.pallas_call(
        _proj_kernel,
        grid=(tokens // PROJ_TILE,),
        in_specs=[pl.BlockSpec((PROJ_TILE, d), lambda i: (i, 0)),
                  pl.BlockSpec((1, d), lambda i: (0, 0)),
                  pl.BlockSpec((d, n), lambda i: (0, 0))],
        out_specs=pl.BlockSpec((PROJ_TILE, n), lambda i: (i, 0)),
        out_shape=jax.ShapeDtypeStruct((tokens, n), F32),
        name="in_proj",
        compiler_params=pltpu.CompilerParams(dimension_semantics=("arbitrary",),
                                             vmem_limit_bytes=VMEM_LIMIT),
    )(x2, norm_w, w)


def _ret_kernel(p_ref, cos_ref, sin_ref, dmat_ref, qdec_ref, ktail_ref, gam_ref, nw_ref,
                o_ref, state_ref):
    first = pl.program_id(1) == 0

    @pl.when(first)
    def _():
        state_ref[...] = jnp.zeros(state_ref.shape, F32)

    def rotary(t):
        blocks = []
        for j in range(t.shape[1] // LANE):
            lanes = slice(LANE * j, LANE * (j + 1))
            tb = t[:, lanes]
            blocks.append(tb * cos_ref[:, lanes]
                          + pltpu.roll(tb, LANE - 1, axis=1) * sin_ref[0, :, lanes]
                          + pltpu.roll(tb, 1, axis=1) * sin_ref[1, :, lanes])
        return jnp.concatenate(blocks, axis=-1)

    q = rotary(p_ref[:, 0:256])
    k = rotary(p_ref[:, 256:512]) * (32.0 ** -0.5)
    v = p_ref[:, 512:1024]
    z = p_ref[:, 1024:1536]
    tile = q.shape[0]
    qmasks = [_lane_mask(LANE, 32 * h, 32 * h + 32) for h in range(4)]
    vmasks = [_lane_mask(256, 64 * h, 64 * h + 64) for h in range(4)]
    n_chunks = tile // CHUNK
    items = [(g, c) for g in range(2) for c in range(n_chunks)]

    def stacked(t, it, width, masks):
        g, c = it
        blk = t[c * CHUNK:(c + 1) * CHUNK, width * g:width * (g + 1)]
        return jnp.concatenate([blk * m for m in masks], axis=0)

    q_st = {it: stacked(q, it, LANE, qmasks) for it in items}
    k_st = {it: stacked(k, it, LANE, qmasks) for it in items}
    v_st = {it: stacked(v, it, 256, vmasks).astype(BF16) for it in items}
    q_dec = {it: (q_st[it] * qdec_ref[it[0]]).astype(BF16) for it in items}
    k_tl = {it: (k_st[it] * ktail_ref[it[0]]).astype(BF16) for it in items}
    s = {it: (_dot_nt(q_st[it], k_st[it]) * dmat_ref[it[0]]).astype(BF16) for it in items}
    y_in = {it: _dot(s[it], v_st[it]) for it in items}
    upd = {it: _dot_tn(k_tl[it], v_st[it]) for it in items}
    y_groups = []
    for g in range(2):
        state = state_ref[g]
        ys = []
        for c in range(n_chunks):
            y_st = y_in[g, c] + _dot(q_dec[g, c], state)
            state = gam_ref[g] * state + upd[g, c]
            ys.append(y_st[0:CHUNK] + y_st[CHUNK:2 * CHUNK]
                      + y_st[2 * CHUNK:3 * CHUNK] + y_st[3 * CHUNK:4 * CHUNK])
        state_ref[g] = state
        y_groups.append(jnp.concatenate(ys, axis=0))
    y = jnp.concatenate(y_groups, axis=-1)
    ms = _seg64_sum(y * y) * (1.0 / 64.0)
    y = y * lax.rsqrt(ms + NORM_EPS) * nw_ref[...]
    o_ref[...] = (y * _silu(z)).astype(o_ref.dtype)


def _ret_tables(seq):
    half = 16
    angle = 1.0 / (ROPE_BASE ** jnp.linspace(0.0, 1.0, half, dtype=F32))
    theta = jnp.arange(seq, dtype=F32)[:, None] * angle[None, :]
    cos = jnp.tile(jnp.repeat(jnp.cos(theta), 2, axis=1), (1, RET_HEADS))
    sin = jnp.tile(jnp.repeat(jnp.sin(theta), 2, axis=1), (1, RET_HEADS))
    even = (jnp.arange(RET_QK) % 2 == 0)[None, :]
    sin = jnp.stack([jnp.where(even, -sin, 0.0), jnp.where(even, 0.0, sin)])
    log_gamma = jnp.log(1.0 - jnp.exp2(-5.0 - jnp.arange(RET_HEADS, dtype=F32)))
    lg = log_gamma.reshape(2, 4)
    i = jnp.arange(CHUNK, dtype=F32)
    diff = i[:, None] - i[None, :]
    low = diff >= 0
    blocks = jnp.exp(jnp.where(low[None, None], lg[:, :, None, None] * diff[None, None], -jnp.inf))
    eye4 = jnp.eye(4, dtype=F32)
    dmat = jnp.einsum('ghij,hk->ghikj', blocks, eye4).reshape(2, 4 * CHUNK, 4 * CHUNK)
    qdec = jnp.exp(lg[:, :, None] * (i[None, None, :] + 1.0)).reshape(2, 4 * CHUNK, 1)
    qdec = jnp.broadcast_to(qdec, (2, 4 * CHUNK, LANE))
    ktail = jnp.exp(lg[:, :, None] * (CHUNK - 1.0 - i[None, None, :])).reshape(2, 4 * CHUNK, 1)
    ktail = jnp.broadcast_to(ktail, (2, 4 * CHUNK, LANE))
    gam_rows = jnp.repeat(jnp.exp(lg * CHUNK), 32, axis=1)
    gam = jnp.broadcast_to(gam_rows[:, :, None], (2, LANE, 256))
    return cos, sin, dmat, qdec, ktail, gam


def _ret_mixer(p, tables, norm_w):
    b, s, n = p.shape
    cos, sin, dmat, qdec, ktail, gam = tables
    full = lambda *shape: pl.BlockSpec(shape, lambda bi, i: (0,) * len(shape))
    return pl.pallas_call(
        _ret_kernel,
        grid=(b, s // LIGHT_TILE),
        in_specs=[pl.BlockSpec((None, LIGHT_TILE, n), lambda bi, i: (bi, i, 0)),
                  pl.BlockSpec((LIGHT_TILE, RET_QK), lambda bi, i: (i, 0)),
                  pl.BlockSpec((2, LIGHT_TILE, RET_QK), lambda bi, i: (0, i, 0)),
                  full(2, 4 * CHUNK, 4 * CHUNK), full(2, 4 * CHUNK, LANE), full(2, 4 * CHUNK, LANE),
                  full(2, LANE, 256), full(1, WIDTH)],
        out_specs=pl.BlockSpec((None, LIGHT_TILE, WIDTH), lambda bi, i: (bi, i, 0)),
        out_shape=jax.ShapeDtypeStruct((b, s, WIDTH), BF16),
        scratch_shapes=[pltpu.VMEM((2, LANE, 256), F32)],
        name="ret_mixer",
        compiler_params=pltpu.CompilerParams(dimension_semantics=("arbitrary", "arbitrary"),
                                             vmem_limit_bytes=VMEM_LIMIT),
    )(p, cos, sin, dmat, qdec, ktail, gam, norm_w)


def _ssd_kernel(p_ref, cw_ref, cb_ref, dtb_ref, a_ref, dskip_ref, nw_ref, e64_ref, e128_ref,
                o_ref, hist_ref, state_ref):
    first = pl.program_id(1) == 0

    @pl.when(first)
    def _():
        state_ref[...] = jnp.zeros(state_ref.shape, F32)

    xbc = _silu(_causal_conv(hist_ref, first, p_ref[:, 0:1024], cw_ref) + cb_ref[...])
    z = p_ref[:, 1024:1536]
    dt = _softplus(p_ref[:, 1536:1664] + dtb_ref[...])
    log_a = dt * a_ref[...]
    tile = xbc.shape[0]
    xs = xbc[:, 0:512]
    g_cum = _chunk_cumsum(log_a, SSD_CHUNK)
    g_all = _select_cols(g_cum, e128_ref[...])
    val = xs * _select_cols(dt, e64_ref[...])
    ri = _iota((SSD_CHUNK, SSD_CHUNK), 0)
    ci = _iota((SSD_CHUNK, SSD_CHUNK), 1)
    low = ri >= ci
    hmask = [_lane_mask(256, 64 * h, 64 * h + 64) for h in range(4)]
    n_chunks = tile // SSD_CHUNK
    items = [(g, c) for g in range(2) for c in range(n_chunks)]
    heads = range(4)

    def rows_of(t, c):
        return t[c * SSD_CHUNK:(c + 1) * SSD_CHUNK]

    bm = {(g, c): rows_of(xbc[:, 512 + LANE * g:512 + LANE * (g + 1)], c) for g, c in items}
    cm = {(g, c): rows_of(xbc[:, 768 + LANE * g:768 + LANE * (g + 1)], c) for g, c in items}
    vc = {(g, c): rows_of(val[:, 256 * g:256 * (g + 1)], c).astype(BF16) for g, c in items}
    gc = {(g, c, h): rows_of(g_all[:, LANE * (4 * g + h):LANE * (4 * g + h + 1)], c)
          for g, c in items for h in heads}
    g_last = {key: t[SSD_CHUNK - 1:SSD_CHUNK, :] for key, t in gc.items()}
    qk = {it: _dot_nt(cm[it], bm[it]) for it in items}
    s_cat = {it: jnp.concatenate(
        [qk[it] * jnp.exp(jnp.where(low, gc[(*it, h)] - gc[(*it, h)].T, -jnp.inf)) for h in heads],
        axis=-1).astype(BF16) for it in items}
    q_cat = {it: jnp.concatenate([cm[it] * jnp.exp(gc[(*it, h)]) for h in heads], axis=-1).astype(BF16)
             for it in items}
    v_st = {it: jnp.concatenate([vc[it] * m.astype(BF16) for m in hmask], axis=0) for it in items}
    y_in = {it: _dot(s_cat[it], v_st[it]) for it in items}
    k_tail = {key: (bm[key[0:2]] * jnp.exp(g_last[key] - t)).astype(BF16) for key, t in gc.items()}
    upd = {key: _dot_tn(k_tail[key], vc[key[0:2]]) * hmask[key[2]] for key in gc}
    y_groups = []
    for g in range(2):
        state = [state_ref[g, SSD_STATE * h:SSD_STATE * (h + 1), :] for h in heads]
        ys = []
        for c in range(n_chunks):
            ys.append(y_in[g, c] + _dot(q_cat[g, c], jnp.concatenate(state, axis=0)))
            state = [state[h] * jnp.concatenate([jnp.exp(g_last[g, c, h])] * 2, axis=-1) + upd[g, c, h]
                     for h in heads]
        for h in heads:
            state_ref[g, SSD_STATE * h:SSD_STATE * (h + 1), :] = state[h]
        y_groups.append(jnp.concatenate(ys, axis=0))
    y = jnp.concatenate(y_groups, axis=-1) + xs * dskip_ref[...]
    y = y * _silu(z)
    outs = []
    for g in range(2):
        yg = y[:, 256 * g:256 * (g + 1)]
        ms = jnp.sum(yg * yg, axis=-1, keepdims=True) * (1.0 / 256.0)
        outs.append(yg * lax.rsqrt(ms + NORM_EPS))
    o_ref[...] = (jnp.concatenate(outs, axis=-1) * nw_ref[...]).astype(o_ref.dtype)


def _ssd_mixer(p, conv_w, conv_b, dt_bias, a_vec, d_skip, norm_w, e64, e128):
    b, s, n = p.shape
    full = lambda *shape: pl.BlockSpec(shape, lambda bi, i: (0,) * len(shape))
    return pl.pallas_call(
        _ssd_kernel,
        grid=(b, s // LIGHT_TILE),
        in_specs=[pl.BlockSpec((None, LIGHT_TILE, n), lambda bi, i: (bi, i, 0)),
                  full(CONV_K, 1024), full(1, 1024), full(1, LANE), full(1, LANE),
                  full(1, WIDTH), full(1, WIDTH), full(LANE, WIDTH), full(LANE, SSD_HEADS * LANE)],
        out_specs=pl.BlockSpec((None, LIGHT_TILE, WIDTH), lambda bi, i: (bi, i, 0)),
        out_shape=jax.ShapeDtypeStruct((b, s, WIDTH), BF16),
        scratch_shapes=[pltpu.VMEM((SUBLANE + LIGHT_TILE, 1024), F32),
                        pltpu.VMEM((2, 4 * SSD_STATE, 256), F32)],
        name="ssd_mixer",
        compiler_params=pltpu.CompilerParams(dimension_semantics=("arbitrary", "arbitrary"),
                                             vmem_limit_bytes=VMEM_LIMIT),
    )(p, conv_w, conv_b, dt_bias, a_vec, d_skip, norm_w, e64, e128)


def _gdn_kernel(p_ref, cw_ref, dtb_ref, a_ref, nw_ref, eb_ref, eg_ref,
                o_ref, hist_ref, state_ref):
    first = pl.program_id(1) == 0

    @pl.when(first)
    def _():
        state_ref[...] = jnp.zeros(state_ref.shape, F32)

    qkv = _silu(_causal_conv(hist_ref, first, p_ref[:, 0:1536], cw_ref))
    z = p_ref[:, 1536:2048]
    ba = p_ref[:, 2048:2176]
    beta_blk = _sigmoid(ba)
    log_a = a_ref[...] * _softplus(ba + dtb_ref[...])
    tile = qkv.shape[0]
    g_all = _select_cols(_chunk_cumsum(log_a, CHUNK), eg_ref[...])
    beta_all = _select_cols(beta_blk, eb_ref[...])

    def l2n(t):
        return t * lax.rsqrt(jnp.sum(t * t, axis=-1, keepdims=True) + 1e-6)

    two = 2 * CHUNK
    ri = _iota((two, two), 0)
    ci = _iota((two, two), 1)
    same = jnp.right_shift(ri, 6) == jnp.right_shift(ci, 6)
    incl = same & (ri >= ci)
    strict = same & (ri > ci)
    n_chunks = tile // CHUNK
    items = [(c, pr) for c in range(n_chunks) for pr in range(2)]

    qh = [l2n(qkv[:, LANE * h:LANE * (h + 1)]) * (GDN_HEAD ** -0.5) for h in range(GDN_HEADS)]
    kh = [l2n(qkv[:, 512 + LANE * h:512 + LANE * (h + 1)]) for h in range(GDN_HEADS)]
    vh = [qkv[:, 1024 + LANE * h:1024 + LANE * (h + 1)] for h in range(GDN_HEADS)]

    def rows_of(ts, it):
        c, pr = it
        return jnp.concatenate([ts[h][c * CHUNK:(c + 1) * CHUNK] for h in (2 * pr, 2 * pr + 1)], axis=0)

    g_heads = [g_all[:, LANE * h:LANE * (h + 1)] for h in range(GDN_HEADS)]
    b_heads = [beta_all[:, LANE * h:LANE * (h + 1)] for h in range(GDN_HEADS)]
    gc = {it: rows_of(g_heads, it) for it in items}
    bt = {it: rows_of(b_heads, it) for it in items}
    q_r = {it: rows_of(qh, it) for it in items}
    k_r = {it: rows_of(kh, it) for it in items}
    v_r = {it: rows_of(vh, it) for it in items}
    kb_r = {it: k_r[it] * bt[it] for it in items}
    eg = {it: jnp.exp(gc[it]) for it in items}
    dec_i = {it: jnp.exp(jnp.where(incl, gc[it] - gc[it].T, -jnp.inf)) for it in items}
    g_last = {it: jnp.concatenate(
        [jnp.broadcast_to(gc[it][CHUNK - 1:CHUNK, :], (CHUNK, LANE)),
         jnp.broadcast_to(gc[it][two - 1:two, :], (CHUNK, LANE))], axis=0) for it in items}
    k_tail = {it: (k_r[it] * jnp.exp(g_last[it] - gc[it])).astype(BF16) for it in items}
    lam = {it: (jnp.exp(gc[it][CHUNK - 1:CHUNK, :]), jnp.exp(gc[it][two - 1:two, :])) for it in items}
    rhs = {it: jnp.concatenate([v_r[it] * bt[it], kb_r[it] * eg[it]], axis=-1).astype(BF16)
           for it in items}
    q_dec = {it: q_r[it] * eg[it] for it in items}
    sc = {it: _dot_nt(jnp.concatenate([kb_r[it], q_r[it]], axis=0), k_r[it]) for it in items}
    ms = [-jnp.where(strict, sc[it][0:two] * dec_i[it], 0.0) for it in items]
    attn = {it: (sc[it][two:2 * two] * dec_i[it]).astype(BF16) for it in items}
    t_invs = dict(zip(items, _neumann_inverses(ms)))
    uw = {it: _dot(t_invs[it], rhs[it]).astype(BF16) for it in items}
    au = {it: _dot(attn[it], uw[it]) for it in items}
    o_0 = {it: au[it][:, 0:LANE] for it in items}
    o_q = {it: (q_dec[it] - au[it][:, LANE:2 * LANE]).astype(BF16) for it in items}
    p_t, q_t = {}, {}
    for c, pr in items:
        for j in range(2):
            hs = slice(j * CHUNK, (j + 1) * CHUNK)
            pq = _dot_tn(k_tail[c, pr][hs],
                         jnp.concatenate([-uw[c, pr][hs, LANE:2 * LANE], uw[c, pr][hs, 0:LANE]], axis=-1))
            p_t[c, 2 * pr + j] = pq[:, 0:LANE].astype(BF16)
            q_t[c, 2 * pr + j] = pq[:, LANE:2 * LANE]

    states = [state_ref[h] for h in range(GDN_HEADS)]
    os = [[] for _ in range(GDN_HEADS)]
    for c in range(n_chunks):
        for h in range(GDN_HEADS):
            pr, j = divmod(h, 2)
            hs = slice(j * CHUNK, (j + 1) * CHUNK)
            os[h].append(_dot(o_q[c, pr][hs], states[h]) + o_0[c, pr][hs])
            states[h] = states[h] * lam[c, pr][j] + _dot(p_t[c, h], states[h]) + q_t[c, h]
    outs = []
    for h in range(GDN_HEADS):
        state_ref[h] = states[h]
        oh = jnp.concatenate(os[h], axis=0)
        ms_h = jnp.sum(oh * oh, axis=-1, keepdims=True) * (1.0 / GDN_HEAD)
        outs.append(oh * lax.rsqrt(ms_h + NORM_EPS) * nw_ref[...])
    o_ref[...] = (jnp.concatenate(outs, axis=-1) * _silu(z)).astype(o_ref.dtype)


def _gdn_mixer(p, conv_w, dt_bias, a_vec, norm_w, e_beta, e_g):
    b, s, n = p.shape
    full = lambda *shape: pl.BlockSpec(shape, lambda bi, i: (0,) * len(shape))
    return pl.pallas_call(
        _gdn_kernel,
        grid=(b, s // TILE),
        in_specs=[pl.BlockSpec((None, TILE, n), lambda bi, i: (bi, i, 0)),
                  full(CONV_K, 1536), full(1, LANE), full(1, LANE), full(1, GDN_HEAD),
                  full(LANE, GDN_HEADS * LANE), full(LANE, GDN_HEADS * LANE)],
        out_specs=pl.BlockSpec((None, TILE, WIDTH), lambda bi, i: (bi, i, 0)),
        out_shape=jax.ShapeDtypeStruct((b, s, WIDTH), BF16),
        scratch_shapes=[pltpu.VMEM((SUBLANE + TILE, 1536), F32),
                        pltpu.VMEM((GDN_HEADS, GDN_HEAD, GDN_HEAD), F32)],
        name="gdn_mixer",
        compiler_params=pltpu.CompilerParams(dimension_semantics=("arbitrary", "arbitrary"),
                                             vmem_limit_bytes=VMEM_LIMIT),
    )(p, conv_w, dt_bias, a_vec, norm_w, e_beta, e_g)


def _rwkv_kernel(p_ref, *refs):
    params, (o_ref, hist_ref, state_ref) = refs[:-3], refs[-3:]
    _interleave([_rwkv_row(p_ref.at[j], *params, o_ref.at[j], hist_ref.at[j], state_ref.at[j])
                 for j in range(p_ref.shape[0])], lead=RWKV_PREP_STEPS)


RWKV_PREP_STEPS = 7


def _rwkv_row(p_ref, mu_ref, wup_ref, w0_ref, aup_ref, a0_ref, kk_ref, ka_ref,
              rk_ref, lnw_ref, lnb_ref, o_ref, hist_ref, state_ref):
    first = pl.program_id(1) == 0

    @pl.when(first)
    def _():
        state_ref[...] = jnp.zeros(state_ref.shape, F32)

    cur = p_ref[:, 0:1664]
    (prev,) = _shifted(hist_ref, first, cur, 1)
    mixed = cur + (prev - cur) * mu_ref[...]
    r = mixed[:, 0:512]
    k = mixed[:, 512:1024]
    v = mixed[:, 1024:1536]
    lo = mixed[:, 1536:1664]
    z = p_ref[:, 1664:2176]
    tile = r.shape[0]
    yield
    logw = -math.exp(-0.5) * _sigmoid(w0_ref[...] + _dot(jnp.tanh(lo), wup_ref[...]))
    iclr = _sigmoid(a0_ref[...] + _dot(lo, aup_ref[...]))
    yield
    kkr = k * kk_ref[...]
    kk = kkr * lax.rsqrt(_seg64_sum(kkr * kkr) + 1e-6)
    k = k * (1.0 + (iclr - 1.0) * ka_ref[...])
    a = -kk
    b = kk * iclr
    yield
    g_inc = _chunk_cumsum(logw, CHUNK)
    g_exc = g_inc - logw
    yield
    e_inc = jnp.exp(g_inc)
    e_neg = jnp.exp(-g_inc)
    r_t = r * e_inc
    a_t = a * jnp.exp(g_exc)
    k_h = k * e_neg
    b_h = b * e_neg
    yield

    two = 2 * CHUNK
    ri = _iota((two, two), 0)
    ci = _iota((two, two), 1)
    same = jnp.right_shift(ri, 6) == jnp.right_shift(ci, 6)
    incl = (same & (ri >= ci)).astype(F32)
    strict = (same & (ri > ci)).astype(F32)
    m0 = _lane_mask(LANE, 0, 64).astype(BF16)
    m1 = _lane_mask(LANE, 64, LANE).astype(BF16)

    def stack(t):
        return jnp.concatenate([t * m0.astype(t.dtype), t * m1.astype(t.dtype)], axis=0)

    def dup(t):
        return jnp.concatenate([t, t], axis=0)

    strict_cat = strict[0:CHUNK] + strict[CHUNK:two]
    incl_cat = incl[0:CHUNK] + incl[CHUNK:two]

    n_chunks = tile // CHUNK
    n_pairs = WIDTH // LANE
    items = [(c, pr) for c in range(n_chunks) for pr in range(n_pairs)]

    def blk(t, it):
        c, pr = it
        return t[c * CHUNK:(c + 1) * CHUNK, LANE * pr:LANE * (pr + 1)]

    a_b, r_b, kh_b, bh_b, v_b = (t.astype(BF16) for t in (a_t, r_t, k_h, b_h, v))
    g_last = {it: g_inc[(it[0] + 1) * CHUNK - 1:(it[0] + 1) * CHUNK, LANE * it[1]:LANE * (it[1] + 1)]
              for it in items}
    tail = {it: jnp.exp(g_last[it] - blk(g_inc, it)) for it in items}
    e_last = {it: jnp.exp(g_last[it]) for it in items}
    a_st = {it: stack(blk(a_b, it)) for it in items}
    r_st = {it: stack(blk(r_b, it)) for it in items}
    yield
    kb_st = {it: jnp.concatenate([stack(blk(kh_b, it)), stack(blk(bh_b, it))], axis=0) for it in items}
    v_st = {it: stack(blk(v_b, it)) for it in items}
    kbt_st = {it: jnp.concatenate([stack((blk(k, it) * tail[it]).astype(BF16)),
                                   stack((blk(b, it) * tail[it]).astype(BF16))], axis=0) for it in items}
    yield
    sc = {it: _dot_nt(jnp.concatenate([blk(a_b, it), blk(r_b, it)], axis=0), kb_st[it]) for it in items}
    yield
    a_abs = [dup(sc[it][0:CHUNK, two:2 * two]) * strict for it in items]
    a_rb = {it: (dup(sc[it][CHUNK:two, two:2 * two]) * incl).astype(BF16) for it in items}
    a_akrk = {it: jnp.concatenate([sc[it][0:CHUNK, 0:two] * strict_cat, sc[it][CHUNK:two, 0:two] * incl_cat],
                                  axis=0).astype(BF16) for it in items}
    yield
    av = {it: _dot(a_akrk[it], v_st[it]) for it in items}
    yield
    t_invs = dict(zip(items, (yield from _neumann_inverse_steps(a_abs))))
    yield
    wu = {it: _dot(t_invs[it], jnp.concatenate([a_st[it], stack(av[it][0:CHUNK].astype(BF16))], axis=-1))
          .astype(BF16) for it in items}
    yield
    ry = {it: jnp.concatenate([r_st[it].astype(F32), stack(av[it][CHUNK:two])], axis=-1)
          + _dot(a_rb[it], wu[it]) for it in items}
    r_q = {it: ry[it][:, 0:LANE].astype(BF16) for it in items}
    y_0 = {it: ry[it][:, LANE:2 * LANE] for it in items}
    yield
    p_l = {it: _dot_tn(wu[it][:, 0:LANE], kbt_st[it][two:2 * two]).astype(BF16) for it in items}
    yield
    q_l = {it: _dot_tn(jnp.concatenate([v_st[it], wu[it][:, LANE:2 * LANE]], axis=0), kbt_st[it])
           for it in items}
    yield

    states = [state_ref[pr] for pr in range(n_pairs)]
    ys = [[] for _ in range(n_pairs)]
    for c in range(n_chunks):
        for pr in range(n_pairs):
            y_st = _dot_nt(r_q[c, pr], states[pr]) + y_0[c, pr]
            states[pr] = states[pr] * e_last[c, pr] + _dot(states[pr], p_l[c, pr]) + q_l[c, pr]
            ys[pr].append(y_st[0:CHUNK] + y_st[CHUNK:two])
    for pr in range(n_pairs):
        state_ref[pr] = states[pr]
    yield
    y = jnp.concatenate([jnp.concatenate(ys[pr], axis=0) for pr in range(n_pairs)], axis=-1)
    mu = _seg64_sum(y) * (1.0 / RW_HEAD)
    yc = y - mu
    yield
    var = _seg64_sum(yc * yc) * (1.0 / RW_HEAD)
    yn = yc * lax.rsqrt(var + RW_GN_EPS) * lnw_ref[...] + lnb_ref[...]
    bonus = _seg64_sum(r * k * rk_ref[...]) * v
    o_ref[...] = ((yn + bonus) * _silu(z)).astype(o_ref.dtype)


def _rwkv_mixer(p, mu, w_up, w0, a_up, a0, k_k, k_a, r_k, ln_w, ln_b):
    b, s, n = p.shape
    full = lambda *shape: pl.BlockSpec(shape, lambda bi, i: (0,) * len(shape))
    return pl.pallas_call(
        _rwkv_kernel,
        grid=(b // ROWS, s // RW_TILE),
        in_specs=[pl.BlockSpec((ROWS, RW_TILE, n), lambda bi, i: (bi, i, 0)),
                  full(1, 1664)] + [full(LANE, WIDTH), full(1, WIDTH)] * 2
                 + [full(1, WIDTH)] * 5,
        out_specs=pl.BlockSpec((ROWS, RW_TILE, WIDTH), lambda bi, i: (bi, i, 0)),
        out_shape=jax.ShapeDtypeStruct((b, s, WIDTH), BF16),
        scratch_shapes=[pltpu.VMEM((ROWS, SUBLANE + RW_TILE, 1664), F32),
                        pltpu.VMEM((ROWS, 4, LANE, LANE), F32)],
        name="rwkv_mixer",
        compiler_params=pltpu.CompilerParams(dimension_semantics=("arbitrary", "arbitrary"),
                                             vmem_limit_bytes=VMEM_LIMIT),
    )(p, mu, w_up, w0, a_up, a0, k_k, k_a, r_k, ln_w, ln_b)


def _merge_kernel(x_ref, nw_ref, wg_ref, ua_ref, ub_ref, uc_ref, ud_ref, wb_ref, wo_ref, fw_ref,
                  o_ref, *, final_norm):
    x = x_ref[...]
    h = (x * lax.rsqrt(jnp.mean(x * x, axis=-1, keepdims=True) + NORM_EPS) * nw_ref[...]).astype(BF16)
    merged = None
    for i, u_ref in enumerate((ua_ref, ub_ref, uc_ref, ud_ref)):
        gate = _sigmoid(jnp.dot(h, wg_ref[:, D_MODEL * i:D_MODEL * (i + 1)],
                                preferred_element_type=F32))
        term = gate * jnp.dot(u_ref[...], wb_ref[i], preferred_element_type=F32)
        merged = term if merged is None else merged + term
    out = x + jnp.dot(merged.astype(BF16), wo_ref[...], preferred_element_type=F32)
    if final_norm:
        out = out * lax.rsqrt(jnp.mean(out * out, axis=-1, keepdims=True) + NORM_EPS) * fw_ref[...]
    o_ref[...] = out


def _merge(x2, norm_w, w_gate, us, w_branch, w_out, final_w, final_norm):
    tokens, d = x2.shape
    row = lambda n: pl.BlockSpec((PROJ_TILE, n), lambda i: (i, 0))
    const = lambda *shape: pl.BlockSpec(shape, lambda i: (0,) * len(shape))
    return pl.pallas_call(
        functools.partial(_merge_kernel, final_norm=final_norm),
        grid=(tokens // PROJ_TILE,),
        in_specs=[row(d), const(1, d), const(d, 4 * d)] + [row(WIDTH)] * 4
                 + [const(4, WIDTH, d), const(d, d), const(1, d)],
        out_specs=row(d),
        out_shape=jax.ShapeDtypeStruct((tokens, d), F32),
        name="merge",
        compiler_params=pltpu.CompilerParams(dimension_semantics=("arbitrary",),
                                             vmem_limit_bytes=VMEM_LIMIT),
    )(x2, norm_w, w_gate, *us, w_branch, w_out, final_w)


def _pad_cols(w, n):
    return jnp.pad(w, ((0, 0), (0, n - w.shape[1])))


def _split_w_in(w_in):
    widths = (WIDTH, WIDTH, WIDTH, RW_LORA, RW_LORA, WIDTH,
              RET_QK, RET_QK, WIDTH, WIDTH,
              WIDTH + 4 * SSD_STATE, WIDTH, SSD_HEADS,
              3 * WIDTH, WIDTH, GDN_HEADS, GDN_HEADS,
              4 * D_MODEL)
    offs = np.cumsum((0,) + widths)
    seg = [w_in[:, int(offs[i]):int(offs[i + 1])] for i in range(len(widths))]
    (rw_r, rw_k, rw_v, rw_wlo, rw_alo, rw_z, rt_q, rt_k, rt_v, rt_z,
     sd_xbc, sd_z, sd_dt, gd_qkv, gd_z, gd_b, gd_a, gates) = seg

    w_rw = jnp.concatenate([rw_r, rw_k, rw_v, rw_wlo, rw_alo, rw_z], axis=1)
    w_rt = jnp.concatenate([rt_q, rt_k, rt_v, rt_z], axis=1)
    w_sd = jnp.concatenate([sd_xbc, sd_z, _pad_cols(sd_dt, LANE)], axis=1)
    w_gd = jnp.concatenate([gd_qkv, gd_z, _pad_cols(jnp.concatenate([gd_b, gd_a], axis=1), LANE)],
                           axis=1)
    return [w.astype(BF16) for w in (w_rw, w_rt, w_sd, w_gd, gates)]


def _head_expand(n_heads, width, offset=0):
    e = np.zeros((LANE, n_heads * width), np.float32)
    for h in range(n_heads):
        e[offset + h, h * width:(h + 1) * width] = 1.0
    return jnp.asarray(e)


def _row(v, n=None):
    v = v.reshape(1, -1).astype(F32)
    return v if n is None else _pad_cols(v, n)


def kernel(x, norm_w, w_in, rwkv_mu_rkv, rwkv_mu_wa, rwkv_w_up, rwkv_w0, rwkv_a_up, rwkv_a0,
           rwkv_k_k, rwkv_k_a, rwkv_r_k, rwkv_ln_w, rwkv_ln_b, ret_norm_w, ssd_conv_w, ssd_conv_b,
           ssd_dt_bias, ssd_A_log, ssd_D, ssd_norm_w, gdn_conv_w, gdn_dt_bias, gdn_A_log, gdn_norm_w,
           w_branch, w_out, final_norm_w):
    b, s, d = x.shape
    depth = norm_w.shape[0]
    tokens = b * s
    ret_tables = _ret_tables(s)
    e64 = _head_expand(SSD_HEADS, 64)
    e128_ssd = _head_expand(SSD_HEADS, LANE)
    e_beta = _head_expand(GDN_HEADS, LANE, 0)
    e_g = _head_expand(GDN_HEADS, LANE, GDN_HEADS)
    zeros_lora = jnp.zeros((RW_LORA, WIDTH), F32)
    x2 = x.reshape(tokens, d)
    for l in range(depth):
        w_rw, w_rt, w_sd, w_gd, w_gate = _split_w_in(w_in[l])
        nw = _row(norm_w[l])
        p_rw = _project(x2, nw, w_rw).reshape(b, s, -1)
        p_rt = _project(x2, nw, w_rt).reshape(b, s, -1)
        p_sd = _project(x2, nw, w_sd).reshape(b, s, -1)
        p_gd = _project(x2, nw, w_gd).reshape(b, s, -1)

        mu = jnp.concatenate([rwkv_mu_rkv[l].reshape(1, -1), rwkv_mu_wa[l].reshape(1, -1)], axis=1)
        w_up = jnp.concatenate([rwkv_w_up[l], zeros_lora], axis=0).astype(BF16)
        a_up = jnp.concatenate([zeros_lora, rwkv_a_up[l]], axis=0).astype(BF16)
        u_a = _rwkv_mixer(p_rw, mu, w_up, _row(rwkv_w0[l]), a_up, _row(rwkv_a0[l]),
                          _row(rwkv_k_k[l]), _row(rwkv_k_a[l]), _row(rwkv_r_k[l]),
                          _row(rwkv_ln_w[l]), _row(rwkv_ln_b[l]))
        u_b = _ret_mixer(p_rt, ret_tables, _row(ret_norm_w[l]))
        u_c = _ssd_mixer(p_sd, ssd_conv_w[l], _row(ssd_conv_b[l]), _row(ssd_dt_bias[l], LANE),
                         _row(-jnp.exp(ssd_A_log[l].astype(F32)), LANE),
                         _row(jnp.repeat(ssd_D[l], 64)), _row(ssd_norm_w[l]), e64, e128_ssd)
        gdn_bias = jnp.concatenate([jnp.zeros((GDN_HEADS,), F32), gdn_dt_bias[l]])
        gdn_a = jnp.concatenate([jnp.zeros((GDN_HEADS,), F32), -jnp.exp(gdn_A_log[l].astype(F32))])
        u_d = _gdn_mixer(p_gd, gdn_conv_w[l], _row(gdn_bias, LANE), _row(gdn_a, LANE),
                         _row(gdn_norm_w[l]), e_beta, e_g)
        us = [u.reshape(tokens, WIDTH) for u in (u_a, u_b, u_c, u_d)]
        x2 = _merge(x2, nw, w_gate, us, w_branch[l].astype(BF16), w_out[l].astype(BF16),
                    _row(final_norm_w), final_norm=(l == depth - 1))
    return x2.reshape(b, s, d)
```

```python
import functools
import math

import numpy as np
import jax
import jax.numpy as jnp
from jax import lax
from jax.experimental import pallas as pl
from jax.experimental.pallas import tpu as pltpu

F32 = jnp.float32
BF16 = jnp.bfloat16

D_MODEL = 1024
WIDTH = 512
NORM_EPS = 1e-6
CONV_K = 4
RW_HEAD = 64
RW_LORA = 64
RW_GN_EPS = 64e-5
RET_HEADS = 8
RET_QK = 256
ROPE_BASE = 10000.0
SSD_HEADS = 8
SSD_STATE = 128
GDN_HEADS = 4
GDN_HEAD = 128

LANE = 128
SUBLANE = 8
CHUNK = 64
SSD_CHUNK = 128
TILE = 512
LIGHT_TILE = 512
RW_TILE = 512
ROWS = 1
PROJ_TILE = 512
PROJ_SUB = 256
VMEM_LIMIT = 56 * 1024 * 1024


def _dot(a, b):
    return jnp.dot(a.astype(BF16), b.astype(BF16), preferred_element_type=F32)


def _dot_nt(a, b):
    return lax.dot_general(a.astype(BF16), b.astype(BF16), (((1,), (1,)), ((), ())),
                           preferred_element_type=F32)


def _dot_tn(a, b):
    return lax.dot_general(a.astype(BF16), b.astype(BF16), (((0,), (0,)), ((), ())),
                           preferred_element_type=F32)


def _bf16_pieces(x):
    hi = x.astype(BF16)
    r1 = x - hi.astype(F32)
    mid = r1.astype(BF16)
    lo = (r1 - mid.astype(F32)).astype(BF16)
    return hi, mid, lo


def _select_rows(sel, x):
    n = x.shape[1]
    y = jnp.dot(sel.astype(BF16), jnp.concatenate(_bf16_pieces(x), axis=-1),
                preferred_element_type=F32)
    return y[:, 0:n] + y[:, n:2 * n] + y[:, 2 * n:3 * n]


def _select_cols(x, sel):
    m = x.shape[0]
    y = jnp.dot(jnp.concatenate(_bf16_pieces(x), axis=0), sel.astype(BF16),
                preferred_element_type=F32)
    return y[0:m] + y[m:2 * m] + y[2 * m:3 * m]


def _sigmoid(x):
    return 1.0 / (1.0 + jnp.exp(-x))


def _silu(x):
    return x * _sigmoid(x)


def _softplus(x):
    return jnp.maximum(x, 0.0) + jnp.log(1.0 + jnp.exp(-jnp.abs(x)))


def _iota(shape, dim):
    return lax.broadcasted_iota(jnp.int32, shape, dim)


def _lane_mask(width, lo, hi):
    lane = _iota((1, width), 1)
    return ((lane >= lo) & (lane < hi)).astype(F32)


def _chunk_cumsum(x, chunk):
    group = 2 * LANE
    r = _iota((group, group), 0)
    c = _iota((group, group), 1)
    sh = int(math.log2(chunk))
    tri = ((r >= c) & (jnp.right_shift(r, sh) == jnp.right_shift(c, sh))).astype(F32)
    parts = [_select_rows(tri, x[i:i + group]) for i in range(0, x.shape[0], group)]
    return parts[0] if len(parts) == 1 else jnp.concatenate(parts, axis=0)


def _seg64_sum(x):
    outs = []
    for j in range(x.shape[1] // LANE):
        xb = x[:, LANE * j:LANE * (j + 1)]
        lo = _iota(xb.shape, 1) < 64
        s_lo = jnp.sum(jnp.where(lo, xb, 0.0), axis=-1, keepdims=True)
        s_hi = jnp.sum(jnp.where(lo, 0.0, xb), axis=-1, keepdims=True)
        outs.append(jnp.where(lo, s_lo, s_hi))
    return outs[0] if len(outs) == 1 else jnp.concatenate(outs, axis=-1)


def _shifted(hist_ref, first, cur, n_shift):
    tile = cur.shape[0]

    @pl.when(first)
    def _():
        hist_ref[0:SUBLANE, :] = jnp.zeros((SUBLANE, cur.shape[1]), F32)

    hist_ref[SUBLANE:SUBLANE + tile, :] = cur
    outs = [hist_ref[SUBLANE - j:SUBLANE - j + tile, :] for j in range(1, n_shift + 1)]
    hist_ref[0:SUBLANE, :] = cur[tile - SUBLANE:tile, :]
    return outs


def _causal_conv(hist_ref, first, cur, w_ref):
    d1, d2, d3 = _shifted(hist_ref, first, cur, CONV_K - 1)
    return (cur * w_ref[3:4, :] + d1 * w_ref[2:3, :] + d2 * w_ref[1:2, :] + d3 * w_ref[0:1, :])


def _interleave(programs, lead):
    live = list(programs)
    for _ in range(lead):
        next(live[0], None)
    while live:
        for prog in list(live):
            try:
                next(prog)
            except StopIteration:
                live.remove(prog)


def _neumann_inverses(ns):
    steps = _neumann_inverse_steps(ns)
    while True:
        try:
            next(steps)
        except StopIteration as done:
            return done.value


def _neumann_inverse_steps(ns):
    size = ns[0].shape[0]
    assert size == LANE and len(ns) % 2 == 0
    eye = (_iota((size, size), 0) == _iota((size, size), 1)).astype(F32)
    zero = jnp.zeros((size, size), BF16)

    def weights(row):
        row = row.astype(BF16)
        return jnp.concatenate([jnp.concatenate([row[:, 0:size], zero], axis=-1),
                                jnp.concatenate([zero, row[:, size:2 * size]], axis=-1)], axis=0)

    ns = [jnp.concatenate([ns[i], ns[i + 1]], axis=-1) for i in range(0, len(ns), 2)]
    eye2 = jnp.concatenate([eye, eye], axis=-1)
    ps = [eye2 + n for n in ns]
    ns = [_dot(n, weights(n)) for n in ns]
    yield
    levels = int(math.log2(CHUNK)) - 1
    for _ in range(levels - 1):
        boths = [_dot(jnp.concatenate([p, n], axis=0), weights(n)) for p, n in zip(ps, ns)]
        ps = [p + both[:size] for p, both in zip(ps, boths)]
        ns = [both[size:] for both in boths]
        yield
    ps = [p + _dot(p, weights(n)) for p, n in zip(ps, ns)]
    return [half for p in ps for half in (p[:, 0:size], p[:, size:2 * size])]


def _proj_kernel(x_ref, nw_ref, w_ref, o_ref):
    for r in range(0, x_ref.shape[0], PROJ_SUB):
        x = x_ref[r:r + PROJ_SUB, :]
        h = x * lax.rsqrt(jnp.mean(x * x, axis=-1, keepdims=True) + NORM_EPS) * nw_ref[...]
        o_ref[r:r + PROJ_SUB, :] = jnp.dot(h.astype(BF16), w_ref[...], preferred_element_type=F32)


def _project(x2, norm_w, w):
    tokens, d = x2.shape
    n = w.shape[1]
    return pl.pallas_call(
        _proj_kernel,
        grid=(tokens // PROJ_TILE,),
        in_specs=[pl.BlockSpec((PROJ_TILE, d), lambda i: (i, 0)),
                  pl.BlockSpec((1, d), lambda i: (0, 0)),
                  pl.BlockSpec((d, n), lambda i: (0, 0))],
        out_specs=pl.BlockSpec((PROJ_TILE, n), lambda i: (i, 0)),
        out_shape=jax.ShapeDtypeStruct((tokens, n), F32),
        name="in_proj",
        compiler_params=pltpu.CompilerParams(dimension_semantics=("arbitrary",),
                                             vmem_limit_bytes=VMEM_LIMIT),
    )(x2, norm_w, w)


def _ret_kernel(p_ref, cos_ref, sin_ref, dmat_ref, qdec_ref, ktail_ref, gam_ref, nw_ref,
                o_ref, state_ref):
    first = pl.program_id(1) == 0

    @pl.when(first)
    def _():
        state_ref[...] = jnp.zeros(state_ref.shape, F32)

    def rotary(t):
        blocks = []
        for j in range(t.shape[1] // LANE):
            lanes = slice(LANE * j, LANE * (j + 1))
            tb = t[:, lanes]
            blocks.append(tb * cos_ref[:, lanes]
                          + pltpu.roll(tb, LANE - 1, axis=1) * sin_ref[0, :, lanes]
                          + pltpu.roll(tb, 1, axis=1) * sin_ref[1, :, lanes])
        return jnp.concatenate(blocks, axis=-1)

    q = rotary(p_ref[:, 0:256])
    k = rotary(p_ref[:, 256:512]) * (32.0 ** -0.5)
    v = p_ref[:, 512:1024]
    z = p_ref[:, 1024:1536]
    tile = q.shape[0]
    qmasks = [_lane_mask(LANE, 32 * h, 32 * h + 32).astype(BF16) for h in range(4)]
    vmasks = [_lane_mask(256, 64 * h, 64 * h + 64).astype(BF16) for h in range(4)]
    n_chunks = tile // CHUNK
    items = [(g, c) for g in range(2) for c in range(n_chunks)]

    def block(t, it, width):
        g, c = it
        return t[c * CHUNK:(c + 1) * CHUNK, width * g:width * (g + 1)]

    def stacked(blk, masks):
        return jnp.concatenate([blk * m for m in masks], axis=0)

    q_b, k_b, v_b = q.astype(BF16), k.astype(BF16), v.astype(BF16)
    q_st = {it: stacked(block(q_b, it, LANE), qmasks) for it in items}
    k_st = {it: stacked(block(k_b, it, LANE), qmasks) for it in items}
    v_st = {it: stacked(block(v_b, it, 256), vmasks) for it in items}
    q_dec = {it: stacked((block(q, it, LANE) * qdec_ref[it[0]]).astype(BF16), qmasks) for it in items}
    k_tl = {it: stacked((block(k, it, LANE) * ktail_ref[it[0]]).astype(BF16), qmasks) for it in items}
    s = {it: (_dot_nt(q_st[it], k_st[it]) * dmat_ref[it[0]]).astype(BF16) for it in items}
    y_in = {it: _dot(s[it], v_st[it]) for it in items}
    upd = {it: _dot_tn(k_tl[it], v_st[it]) for it in items}
    y_groups = []
    for g in range(2):
        state = state_ref[g]
        ys = []
        for c in range(n_chunks):
            y_st = y_in[g, c] + _dot(q_dec[g, c], state)
            state = gam_ref[g] * state + upd[g, c]
            ys.append(y_st[0:CHUNK] + y_st[CHUNK:2 * CHUNK]
                      + y_st[2 * CHUNK:3 * CHUNK] + y_st[3 * CHUNK:4 * CHUNK])
        state_ref[g] = state
        y_groups.append(jnp.concatenate(ys, axis=0))
    y = jnp.concatenate(y_groups, axis=-1)
    ms = _seg64_sum(y * y) * (1.0 / 64.0)
    y = y * lax.rsqrt(ms + NORM_EPS) * nw_ref[...]
    o_ref[...] = (y * _silu(z)).astype(o_ref.dtype)


def _ret_tables(seq):
    half = 16
    angle = 1.0 / (ROPE_BASE ** jnp.linspace(0.0, 1.0, half, dtype=F32))
    theta = jnp.arange(seq, dtype=F32)[:, None] * angle[None, :]
    cos = jnp.tile(jnp.repeat(jnp.cos(theta), 2, axis=1), (1, RET_HEADS))
    sin = jnp.tile(jnp.repeat(jnp.sin(theta), 2, axis=1), (1, RET_HEADS))
    even = (jnp.arange(RET_QK) % 2 == 0)[None, :]
    sin = jnp.stack([jnp.where(even, -sin, 0.0), jnp.where(even, 0.0, sin)])
    log_gamma = jnp.log(1.0 - jnp.exp2(-5.0 - jnp.arange(RET_HEADS, dtype=F32)))
    lg = log_gamma.reshape(2, 4)
    i = jnp.arange(CHUNK, dtype=F32)
    diff = i[:, None] - i[None, :]
    low = diff >= 0
    blocks = jnp.exp(jnp.where(low[None, None], lg[:, :, None, None] * diff[None, None], -jnp.inf))
    eye4 = jnp.eye(4, dtype=F32)
    dmat = jnp.einsum('ghij,hk->ghikj', blocks, eye4).reshape(2, 4 * CHUNK, 4 * CHUNK)
    lg_lane = jnp.repeat(lg, 32, axis=1)
    qdec = jnp.exp(lg_lane[:, None, :] * (i[None, :, None] + 1.0))
    ktail = jnp.exp(lg_lane[:, None, :] * (CHUNK - 1.0 - i[None, :, None]))
    gam_rows = jnp.repeat(jnp.exp(lg * CHUNK), 32, axis=1)
    gam = jnp.broadcast_to(gam_rows[:, :, None], (2, LANE, 256))
    return cos, sin, dmat, qdec, ktail, gam


def _ret_mixer(p, tables, norm_w):
    b, s, n = p.shape
    cos, sin, dmat, qdec, ktail, gam = tables
    full = lambda *shape: pl.BlockSpec(shape, lambda bi, i: (0,) * len(shape))
    return pl.pallas_call(
        _ret_kernel,
        grid=(b, s // LIGHT_TILE),
        in_specs=[pl.BlockSpec((None, LIGHT_TILE, n), lambda bi, i: (bi, i, 0)),
                  pl.BlockSpec((LIGHT_TILE, RET_QK), lambda bi, i: (i, 0)),
                  pl.BlockSpec((2, LIGHT_TILE, RET_QK), lambda bi, i: (0, i, 0)),
                  full(2, 4 * CHUNK, 4 * CHUNK), full(2, CHUNK, LANE), full(2, CHUNK, LANE),
                  full(2, LANE, 256), full(1, WIDTH)],
        out_specs=pl.BlockSpec((None, LIGHT_TILE, WIDTH), lambda bi, i: (bi, i, 0)),
        out_shape=jax.ShapeDtypeStruct((b, s, WIDTH), BF16),
        scratch_shapes=[pltpu.VMEM((2, LANE, 256), F32)],
        name="ret_mixer",
        compiler_params=pltpu.CompilerParams(dimension_semantics=("arbitrary", "arbitrary"),
                                             vmem_limit_bytes=VMEM_LIMIT),
    )(p, cos, sin, dmat, qdec, ktail, gam, norm_w)


def _ssd_kernel(p_ref, cw_ref, cb_ref, dtb_ref, a_ref, dskip_ref, nw_ref, e64_ref, e128_ref,
                o_ref, hist_ref, state_ref):
    first = pl.program_id(1) == 0

    @pl.when(first)
    def _():
        state_ref[...] = jnp.zeros(state_ref.shape, F32)

    xbc = _silu(_causal_conv(hist_ref, first, p_ref[:, 0:1024], cw_ref) + cb_ref[...])
    z = p_ref[:, 1024:1536]
    dt = _softplus(p_ref[:, 1536:1664] + dtb_ref[...])
    log_a = dt * a_ref[...]
    tile = xbc.shape[0]
    xs = xbc[:, 0:512]
    g_cum = _chunk_cumsum(log_a, SSD_CHUNK)
    g_all = _select_cols(g_cum, e128_ref[...])
    val = xs * _select_cols(dt, e64_ref[...])
    ri = _iota((SSD_CHUNK, SSD_CHUNK), 0)
    ci = _iota((SSD_CHUNK, SSD_CHUNK), 1)
    low = ri >= ci
    hmask = [_lane_mask(256, 64 * h, 64 * h + 64) for h in range(4)]
    n_chunks = tile // SSD_CHUNK
    items = [(g, c) for g in range(2) for c in range(n_chunks)]
    heads = range(4)

    def rows_of(t, c):
        return t[c * SSD_CHUNK:(c + 1) * SSD_CHUNK]

    bm = {(g, c): rows_of(xbc[:, 512 + LANE * g:512 + LANE * (g + 1)], c) for g, c in items}
    cm = {(g, c): rows_of(xbc[:, 768 + LANE * g:768 + LANE * (g + 1)], c) for g, c in items}
    vc = {(g, c): rows_of(val[:, 256 * g:256 * (g + 1)], c).astype(BF16) for g, c in items}
    gc = {(g, c, h): rows_of(g_all[:, LANE * (4 * g + h):LANE * (4 * g + h + 1)], c)
          for g, c in items for h in heads}
    g_last = {key: t[SSD_CHUNK - 1:SSD_CHUNK, :] for key, t in gc.items()}
    qk = {it: _dot_nt(cm[it], bm[it]) for it in items}
    s_cat = {it: jnp.concatenate(
        [qk[it] * jnp.exp(jnp.where(low, gc[(*it, h)] - gc[(*it, h)].T, -jnp.inf)) for h in heads],
        axis=-1).astype(BF16) for it in items}
    q_cat = {it: jnp.concatenate([cm[it] * jnp.exp(gc[(*it, h)]) for h in heads], axis=-1).astype(BF16)
             for it in items}
    v_st = {it: jnp.concatenate([vc[it] * m.astype(BF16) for m in hmask], axis=0) for it in items}
    y_in = {it: _dot(s_cat[it], v_st[it]) for it in items}
    k_tail = {key: (bm[key[0:2]] * jnp.exp(g_last[key] - t)).astype(BF16) for key, t in gc.items()}
    upd = {key: _dot_tn(k_tail[key], vc[key[0:2]]) * hmask[key[2]] for key in gc}
    y_groups = []
    for g in range(2):
        state = [state_ref[g, SSD_STATE * h:SSD_STATE * (h + 1), :] for h in heads]
        ys = []
        for c in range(n_chunks):
            ys.append(y_in[g, c] + _dot(q_cat[g, c], jnp.concatenate(state, axis=0)))
            state = [state[h] * jnp.concatenate([jnp.exp(g_last[g, c, h])] * 2, axis=-1) + upd[g, c, h]
                     for h in heads]
        for h in heads:
            state_ref[g, SSD_STATE * h:SSD_STATE * (h + 1), :] = state[h]
        y_groups.append(jnp.concatenate(ys, axis=0))
    y = jnp.concatenate(y_groups, axis=-1) + xs * dskip_ref[...]
    y = y * _silu(z)
    outs = []
    for g in range(2):
        yg = y[:, 256 * g:256 * (g + 1)]
        ms = jnp.sum(yg * yg, axis=-1, keepdims=True) * (1.0 / 256.0)
        outs.append(yg * lax.rsqrt(ms + NORM_EPS))
    o_ref[...] = (jnp.concatenate(outs, axis=-1) * nw_ref[...]).astype(o_ref.dtype)


def _ssd_mixer(p, conv_w, conv_b, dt_bias, a_vec, d_skip, norm_w, e64, e128):
    b, s, n = p.shape
    full = lambda *shape: pl.BlockSpec(shape, lambda bi, i: (0,) * len(shape))
    return pl.pallas_call(
        _ssd_kernel,
        grid=(b, s // LIGHT_TILE),
        in_specs=[pl.BlockSpec((None, LIGHT_TILE, n), lambda bi, i: (bi, i, 0)),
                  full(CONV_K, 1024), full(1, 1024), full(1, LANE), full(1, LANE),
                  full(1, WIDTH), full(1, WIDTH), full(LANE, WIDTH), full(LANE, SSD_HEADS * LANE)],
        out_specs=pl.BlockSpec((None, LIGHT_TILE, WIDTH), lambda bi, i: (bi, i, 0)),
        out_shape=jax.ShapeDtypeStruct((b, s, WIDTH), BF16),
        scratch_shapes=[pltpu.VMEM((SUBLANE + LIGHT_TILE, 1024), F32),
                        pltpu.VMEM((2, 4 * SSD_STATE, 256), F32)],
        name="ssd_mixer",
        compiler_params=pltpu.CompilerParams(dimension_semantics=("arbitrary", "arbitrary"),
                                             vmem_limit_bytes=VMEM_LIMIT),
    )(p, conv_w, conv_b, dt_bias, a_vec, d_skip, norm_w, e64, e128)


def _gdn_kernel(p_ref, cw_ref, dtb_ref, a_ref, nw_ref, eb_ref, eg_ref,
                o_ref, hist_ref, state_ref):
    first = pl.program_id(1) == 0

    @pl.when(first)
    def _():
        state_ref[...] = jnp.zeros(state_ref.shape, F32)

    qkv = _silu(_causal_conv(hist_ref, first, p_ref[:, 0:1536], cw_ref))
    z = p_ref[:, 1536:2048]
    ba = p_ref[:, 2048:2176]
    beta_blk = _sigmoid(ba)
    log_a = a_ref[...] * _softplus(ba + dtb_ref[...])
    tile = qkv.shape[0]
    g_all = _select_cols(_chunk_cumsum(log_a, CHUNK), eg_ref[...])
    beta_all = _select_cols(beta_blk, eb_ref[...])

    def l2n(t):
        return t * lax.rsqrt(jnp.sum(t * t, axis=-1, keepdims=True) + 1e-6)

    two = 2 * CHUNK
    ri = _iota((two, two), 0)
    ci = _iota((two, two), 1)
    same = jnp.right_shift(ri, 6) == jnp.right_shift(ci, 6)
    incl = same & (ri >= ci)
    strict = same & (ri > ci)
    n_chunks = tile // CHUNK
    items = [(c, pr) for c in range(n_chunks) for pr in range(2)]

    qh = [l2n(qkv[:, LANE * h:LANE * (h + 1)]) * (GDN_HEAD ** -0.5) for h in range(GDN_HEADS)]
    kh = [l2n(qkv[:, 512 + LANE * h:512 + LANE * (h + 1)]) for h in range(GDN_HEADS)]
    vh = [qkv[:, 1024 + LANE * h:1024 + LANE * (h + 1)] for h in range(GDN_HEADS)]

    def rows_of(ts, it):
        c, pr = it
        return jnp.concatenate([ts[h][c * CHUNK:(c + 1) * CHUNK] for h in (2 * pr, 2 * pr + 1)], axis=0)

    g_heads = [g_all[:, LANE * h:LANE * (h + 1)] for h in range(GDN_HEADS)]
    b_heads = [beta_all[:, LANE * h:LANE * (h + 1)] for h in range(GDN_HEADS)]
    gc = {it: rows_of(g_heads, it) for it in items}
    bt = {it: rows_of(b_heads, it) for it in items}
    q_r = {it: rows_of(qh, it) for it in items}
    k_r = {it: rows_of(kh, it) for it in items}
    v_r = {it: rows_of(vh, it) for it in items}
    kb_r = {it: k_r[it] * bt[it] for it in items}
    eg = {it: jnp.exp(gc[it]) for it in items}
    dec_i = {it: jnp.exp(jnp.where(incl, gc[it] - gc[it].T, -jnp.inf)) for it in items}
    g_last = {it: jnp.concatenate(
        [jnp.broadcast_to(gc[it][CHUNK - 1:CHUNK, :], (CHUNK, LANE)),
         jnp.broadcast_to(gc[it][two - 1:two, :], (CHUNK, LANE))], axis=0) for it in items}
    k_tail = {it: (k_r[it] * jnp.exp(g_last[it] - gc[it])).astype(BF16) for it in items}
    lam = {it: (jnp.exp(gc[it][CHUNK - 1:CHUNK, :]), jnp.exp(gc[it][two - 1:two, :])) for it in items}
    rhs = {it: jnp.concatenate([v_r[it] * bt[it], kb_r[it] * eg[it]], axis=-1).astype(BF16)
           for it in items}
    q_dec = {it: q_r[it] * eg[it] for it in items}
    sc = {it: _dot_nt(jnp.concatenate([kb_r[it], q_r[it]], axis=0), k_r[it]) for it in items}
    ms = [-jnp.where(strict, sc[it][0:two] * dec_i[it], 0.0) for it in items]
    attn = {it: (sc[it][two:2 * two] * dec_i[it]).astype(BF16) for it in items}
    t_invs = dict(zip(items, _neumann_inverses(ms)))
    uw = {it: _dot(t_invs[it], rhs[it]).astype(BF16) for it in items}
    au = {it: _dot(attn[it], uw[it]) for it in items}
    o_0 = {it: au[it][:, 0:LANE] for it in items}
    o_q = {it: (q_dec[it] - au[it][:, LANE:2 * LANE]).astype(BF16) for it in items}
    p_t, q_t = {}, {}
    for c, pr in items:
        for j in range(2):
            hs = slice(j * CHUNK, (j + 1) * CHUNK)
            pq = _dot_tn(k_tail[c, pr][hs],
                         jnp.concatenate([-uw[c, pr][hs, LANE:2 * LANE], uw[c, pr][hs, 0:LANE]], axis=-1))
            p_t[c, 2 * pr + j] = pq[:, 0:LANE].astype(BF16)
            q_t[c, 2 * pr + j] = pq[:, LANE:2 * LANE]

    states = [state_ref[h] for h in range(GDN_HEADS)]
    os = [[] for _ in range(GDN_HEADS)]
    for c in range(n_chunks):
        for h in range(GDN_HEADS):
            pr, j = divmod(h, 2)
            hs = slice(j * CHUNK, (j + 1) * CHUNK)
            os[h].append(_dot(o_q[c, pr][hs], states[h]) + o_0[c, pr][hs])
            states[h] = states[h] * lam[c, pr][j] + _dot(p_t[c, h], states[h]) + q_t[c, h]
    outs = []
    for h in range(GDN_HEADS):
        state_ref[h] = states[h]
        oh = jnp.concatenate(os[h], axis=0)
        ms_h = jnp.sum(oh * oh, axis=-1, keepdims=True) * (1.0 / GDN_HEAD)
        outs.append(oh * lax.rsqrt(ms_h + NORM_EPS) * nw_ref[...])
    o_ref[...] = (jnp.concatenate(outs, axis=-1) * _silu(z)).astype(o_ref.dtype)


def _gdn_mixer(p, conv_w, dt_bias, a_vec, norm_w, e_beta, e_g):
    b, s, n = p.shape
    full = lambda *shape: pl.BlockSpec(shape, lambda bi, i: (0,) * len(shape))
    return pl.pallas_call(
        _gdn_kernel,
        grid=(b, s // TILE),
        in_specs=[pl.BlockSpec((None, TILE, n), lambda bi, i: (bi, i, 0)),
                  full(CONV_K, 1536), full(1, LANE), full(1, LANE), full(1, GDN_HEAD),
                  full(LANE, GDN_HEADS * LANE), full(LANE, GDN_HEADS * LANE)],
        out_specs=pl.BlockSpec((None, TILE, WIDTH), lambda bi, i: (bi, i, 0)),
        out_shape=jax.ShapeDtypeStruct((b, s, WIDTH), BF16),
        scratch_shapes=[pltpu.VMEM((SUBLANE + TILE, 1536), F32),
                        pltpu.VMEM((GDN_HEADS, GDN_HEAD, GDN_HEAD), F32)],
        name="gdn_mixer",
        compiler_params=pltpu.CompilerParams(dimension_semantics=("arbitrary", "arbitrary"),
                                             vmem_limit_bytes=VMEM_LIMIT),
    )(p, conv_w, dt_bias, a_vec, norm_w, e_beta, e_g)


def _rwkv_kernel(p_ref, *refs):
    params, (o_ref, hist_ref, state_ref) = refs[:-3], refs[-3:]
    _interleave([_rwkv_row(p_ref.at[j], *params, o_ref.at[j], hist_ref.at[j], state_ref.at[j])
                 for j in range(p_ref.shape[0])], lead=RWKV_PREP_STEPS)


RWKV_PREP_STEPS = 7


def _rwkv_row(p_ref, mu_ref, wup_ref, w0_ref, aup_ref, a0_ref, kk_ref, ka_ref,
              rk_ref, lnw_ref, lnb_ref, o_ref, hist_ref, state_ref):
    first = pl.program_id(1) == 0

    @pl.when(first)
    def _():
        state_ref[...] = jnp.zeros(state_ref.shape, F32)

    cur = p_ref[:, 0:1664]
    (prev,) = _shifted(hist_ref, first, cur, 1)
    mixed = cur + (prev - cur) * mu_ref[...]
    r = mixed[:, 0:512]
    k = mixed[:, 512:1024]
    v = mixed[:, 1024:1536]
    lo = mixed[:, 1536:1664]
    z = p_ref[:, 1664:2176]
    tile = r.shape[0]
    yield
    logw = -math.exp(-0.5) * _sigmoid(w0_ref[...] + _dot(jnp.tanh(lo), wup_ref[...]))
    iclr = _sigmoid(a0_ref[...] + _dot(lo, aup_ref[...]))
    yield
    kkr = k * kk_ref[...]
    kk = kkr * lax.rsqrt(_seg64_sum(kkr * kkr) + 1e-6)
    k = k * (1.0 + (iclr - 1.0) * ka_ref[...])
    a = -kk
    b = kk * iclr
    yield
    g_inc = _chunk_cumsum(logw, CHUNK)
    g_exc = g_inc - logw
    yield
    e_inc = jnp.exp(g_inc)
    e_neg = jnp.exp(-g_inc)
    r_t = r * e_inc
    a_t = a * jnp.exp(g_exc)
    k_h = k * e_neg
    b_h = b * e_neg
    yield

    two = 2 * CHUNK
    ri = _iota((two, two), 0)
    ci = _iota((two, two), 1)
    same = jnp.right_shift(ri, 6) == jnp.right_shift(ci, 6)
    incl = (same & (ri >= ci)).astype(F32)
    strict = (same & (ri > ci)).astype(F32)
    m0 = _lane_mask(LANE, 0, 64).astype(BF16)
    m1 = _lane_mask(LANE, 64, LANE).astype(BF16)

    def stack(t):
        return jnp.concatenate([t * m0.astype(t.dtype), t * m1.astype(t.dtype)], axis=0)

    def dup(t):
        return jnp.concatenate([t, t], axis=0)

    strict_cat = strict[0:CHUNK] + strict[CHUNK:two]
    incl_cat = incl[0:CHUNK] + incl[CHUNK:two]

    n_chunks = tile // CHUNK
    n_pairs = WIDTH // LANE
    items = [(c, pr) for c in range(n_chunks) for pr in range(n_pairs)]

    def blk(t, it):
        c, pr = it
        return t[c * CHUNK:(c + 1) * CHUNK, LANE * pr:LANE * (pr + 1)]

    a_b, r_b, kh_b, bh_b, v_b = (t.astype(BF16) for t in (a_t, r_t, k_h, b_h, v))
    g_last = {it: g_inc[(it[0] + 1) * CHUNK - 1:(it[0] + 1) * CHUNK, LANE * it[1]:LANE * (it[1] + 1)]
              for it in items}
    tail = {it: jnp.exp(g_last[it] - blk(g_inc, it)) for it in items}
    e_last = {it: jnp.exp(g_last[it]) for it in items}
    a_st = {it: stack(blk(a_b, it)) for it in items}
    r_st = {it: stack(blk(r_b, it)) for it in items}
    yield
    kb_st = {it: jnp.concatenate([stack(blk(kh_b, it)), stack(blk(bh_b, it))], axis=0) for it in items}
    v_st = {it: stack(blk(v_b, it)) for it in items}
    kbt_st = {it: jnp.concatenate([stack((blk(k, it) * tail[it]).astype(BF16)),
                                   stack((blk(b, it) * tail[it]).astype(BF16))], axis=0) for it in items}
    yield
    sc = {it: _dot_nt(jnp.concatenate([blk(a_b, it), blk(r_b, it)], axis=0), kb_st[it]) for it in items}
    yield
    a_abs = [dup(sc[it][0:CHUNK, two:2 * two]) * strict for it in items]
    a_rb = {it: (dup(sc[it][CHUNK:two, two:2 * two]) * incl).astype(BF16) for it in items}
    a_akrk = {it: jnp.concatenate([sc[it][0:CHUNK, 0:two] * strict_cat, sc[it][CHUNK:two, 0:two] * incl_cat],
                                  axis=0).astype(BF16) for it in items}
    yield
    av = {it: _dot(a_akrk[it], v_st[it]) for it in items}
    yield
    t_invs = dict(zip(items, (yield from _neumann_inverse_steps(a_abs))))
    yield
    wu = {it: _dot(t_invs[it], jnp.concatenate([a_st[it], stack(av[it][0:CHUNK].astype(BF16))], axis=-1))
          .astype(BF16) for it in items}
    yield
    ry = {it: jnp.concatenate([r_st[it].astype(F32), stack(av[it][CHUNK:two])], axis=-1)
          + _dot(a_rb[it], wu[it]) for it in items}
    r_q = {it: ry[it][:, 0:LANE].astype(BF16) for it in items}
    y_0 = {it: ry[it][:, LANE:2 * LANE] for it in items}
    yield
    p_l = {it: _dot_tn(wu[it][:, 0:LANE], kbt_st[it][two:2 * two]).astype(BF16) for it in items}
    yield
    q_l = {it: _dot_tn(jnp.concatenate([v_st[it], wu[it][:, LANE:2 * LANE]], axis=0), kbt_st[it])
           for it in items}
    yield

    states = [state_ref[pr] for pr in range(n_pairs)]
    ys = [[] for _ in range(n_pairs)]
    for c in range(n_chunks):
        for pr in range(n_pairs):
            y_st = _dot_nt(r_q[c, pr], states[pr]) + y_0[c, pr]
            states[pr] = states[pr] * e_last[c, pr] + _dot(states[pr], p_l[c, pr]) + q_l[c, pr]
            ys[pr].append(y_st[0:CHUNK] + y_st[CHUNK:two])
    for pr in range(n_pairs):
        state_ref[pr] = states[pr]
    yield
    y = jnp.concatenate([jnp.concatenate(ys[pr], axis=0) for pr in range(n_pairs)], axis=-1)
    mu = _seg64_sum(y) * (1.0 / RW_HEAD)
    yc = y - mu
    yield
    var = _seg64_sum(yc * yc) * (1.0 / RW_HEAD)
    yn = yc * lax.rsqrt(var + RW_GN_EPS) * lnw_ref[...] + lnb_ref[...]
    bonus = _seg64_sum(r * k * rk_ref[...]) * v
    o_ref[...] = ((yn + bonus) * _silu(z)).astype(o_ref.dtype)


def _rwkv_mixer(p, mu, w_up, w0, a_up, a0, k_k, k_a, r_k, ln_w, ln_b):
    b, s, n = p.shape
    full = lambda *shape: pl.BlockSpec(shape, lambda bi, i: (0,) * len(shape))
    return pl.pallas_call(
        _rwkv_kernel,
        grid=(b // ROWS, s // RW_TILE),
        in_specs=[pl.BlockSpec((ROWS, RW_TILE, n), lambda bi, i: (bi, i, 0)),
                  full(1, 1664)] + [full(LANE, WIDTH), full(1, WIDTH)] * 2
                 + [full(1, WIDTH)] * 5,
        out_specs=pl.BlockSpec((ROWS, RW_TILE, WIDTH), lambda bi, i: (bi, i, 0)),
        out_shape=jax.ShapeDtypeStruct((b, s, WIDTH), BF16),
        scratch_shapes=[pltpu.VMEM((ROWS, SUBLANE + RW_TILE, 1664), F32),
                        pltpu.VMEM((ROWS, 4, LANE, LANE), F32)],
        name="rwkv_mixer",
        compiler_params=pltpu.CompilerParams(dimension_semantics=("arbitrary", "arbitrary"),
                                             vmem_limit_bytes=VMEM_LIMIT),
    )(p, mu, w_up, w0, a_up, a0, k_k, k_a, r_k, ln_w, ln_b)


def _merge_kernel(x_ref, nw_ref, wg_ref, ua_ref, ub_ref, uc_ref, ud_ref, wb_ref, wo_ref, fw_ref,
                  o_ref, *, final_norm):
    for r in range(0, x_ref.shape[0], PROJ_SUB):
        rows = slice(r, r + PROJ_SUB)
        x = x_ref[rows, :]
        h = (x * lax.rsqrt(jnp.mean(x * x, axis=-1, keepdims=True) + NORM_EPS) * nw_ref[...]).astype(BF16)
        merged = None
        for i, u_ref in enumerate((ua_ref, ub_ref, uc_ref, ud_ref)):
            gate = _sigmoid(jnp.dot(h, wg_ref[:, D_MODEL * i:D_MODEL * (i + 1)],
                                    preferred_element_type=F32))
            term = gate * jnp.dot(u_ref[rows, :], wb_ref[i], preferred_element_type=F32)
            merged = term if merged is None else merged + term
        out = x + jnp.dot(merged.astype(BF16), wo_ref[...], preferred_element_type=F32)
        if final_norm:
            out = out * lax.rsqrt(jnp.mean(out * out, axis=-1, keepdims=True) + NORM_EPS) * fw_ref[...]
        o_ref[rows, :] = out


def _merge(x2, norm_w, w_gate, us, w_branch, w_out, final_w, final_norm):
    tokens, d = x2.shape
    row = lambda n: pl.BlockSpec((PROJ_TILE, n), lambda i: (i, 0))
    const = lambda *shape: pl.BlockSpec(shape, lambda i: (0,) * len(shape))
    return pl.pallas_call(
        functools.partial(_merge_kernel, final_norm=final_norm),
        grid=(tokens // PROJ_TILE,),
        in_specs=[row(d), const(1, d), const(d, 4 * d)] + [row(WIDTH)] * 4
                 + [const(4, WIDTH, d), const(d, d), const(1, d)],
        out_specs=row(d),
        out_shape=jax.ShapeDtypeStruct((tokens, d), F32),
        name="merge",
        compiler_params=pltpu.CompilerParams(dimension_semantics=("arbitrary",),
                                             vmem_limit_bytes=VMEM_LIMIT),
    )(x2, norm_w, w_gate, *us, w_branch, w_out, final_w)


def _pad_cols(w, n):
    return jnp.pad(w, ((0, 0), (0, n - w.shape[1])))


def _split_w_in(w_in):
    widths = (WIDTH, WIDTH, WIDTH, RW_LORA, RW_LORA, WIDTH,
              RET_QK, RET_QK, WIDTH, WIDTH,
              WIDTH + 4 * SSD_STATE, WIDTH, SSD_HEADS,
              3 * WIDTH, WIDTH, GDN_HEADS, GDN_HEADS,
              4 * D_MODEL)
    offs = np.cumsum((0,) + widths)
    seg = [w_in[:, int(offs[i]):int(offs[i + 1])] for i in range(len(widths))]
    (rw_r, rw_k, rw_v, rw_wlo, rw_alo, rw_z, rt_q, rt_k, rt_v, rt_z,
     sd_xbc, sd_z, sd_dt, gd_qkv, gd_z, gd_b, gd_a, gates) = seg

    w_rw = jnp.concatenate([rw_r, rw_k, rw_v, rw_wlo, rw_alo, rw_z], axis=1)
    w_rt = jnp.concatenate([rt_q, rt_k, rt_v, rt_z], axis=1)
    w_sd = jnp.concatenate([sd_xbc, sd_z, _pad_cols(sd_dt, LANE)], axis=1)
    w_gd = jnp.concatenate([gd_qkv, gd_z, _pad_cols(jnp.concatenate([gd_b, gd_a], axis=1), LANE)],
                           axis=1)
    return [w.astype(BF16) for w in (w_rw, w_rt, w_sd, w_gd, gates)]


def _head_expand(n_heads, width, offset=0):
    e = np.zeros((LANE, n_heads * width), np.float32)
    for h in range(n_heads):
        e[offset + h, h * width:(h + 1) * width] = 1.0
    return jnp.asarray(e)


def _row(v, n=None):
    v = v.reshape(1, -1).astype(F32)
    return v if n is None else _pad_cols(v, n)


def kernel(x, norm_w, w_in, rwkv_mu_rkv, rwkv_mu_wa, rwkv_w_up, rwkv_w0, rwkv_a_up, rwkv_a0,
           rwkv_k_k, rwkv_k_a, rwkv_r_k, rwkv_ln_w, rwkv_ln_b, ret_norm_w, ssd_conv_w, ssd_conv_b,
           ssd_dt_bias, ssd_A_log, ssd_D, ssd_norm_w, gdn_conv_w, gdn_dt_bias, gdn_A_log, gdn_norm_w,
           w_branch, w_out, final_norm_w):
    b, s, d = x.shape
    depth = norm_w.shape[0]
    tokens = b * s
    ret_tables = _ret_tables(s)
    e64 = _head_expand(SSD_HEADS, 64)
    e128_ssd = _head_expand(SSD_HEADS, LANE)
    e_beta = _head_expand(GDN_HEADS, LANE, 0)
    e_g = _head_expand(GDN_HEADS, LANE, GDN_HEADS)
    zeros_lora = jnp.zeros((RW_LORA, WIDTH), F32)
    x2 = x.reshape(tokens, d)
    for l in range(depth):
        w_rw, w_rt, w_sd, w_gd, w_gate = _split_w_in(w_in[l])
        nw = _row(norm_w[l])
        p_rw = _project(x2, nw, w_rw).reshape(b, s, -1)
        p_rt = _project(x2, nw, w_rt).reshape(b, s, -1)
        p_sd = _project(x2, nw, w_sd).reshape(b, s, -1)
        p_gd = _project(x2, nw, w_gd).reshape(b, s, -1)

        mu = jnp.concatenate([rwkv_mu_rkv[l].reshape(1, -1), rwkv_mu_wa[l].reshape(1, -1)], axis=1)
        w_up = jnp.concatenate([rwkv_w_up[l], zeros_lora], axis=0).astype(BF16)
        a_up = jnp.concatenate([zeros_lora, rwkv_a_up[l]], axis=0).astype(BF16)
        u_a = _rwkv_mixer(p_rw, mu, w_up, _row(rwkv_w0[l]), a_up, _row(rwkv_a0[l]),
                          _row(rwkv_k_k[l]), _row(rwkv_k_a[l]), _row(rwkv_r_k[l]),
                          _row(rwkv_ln_w[l]), _row(rwkv_ln_b[l]))
        u_b = _ret_mixer(p_rt, ret_tables, _row(ret_norm_w[l]))
        u_c = _ssd_mixer(p_sd, ssd_conv_w[l], _row(ssd_conv_b[l]), _row(ssd_dt_bias[l], LANE),
                         _row(-jnp.exp(ssd_A_log[l].astype(F32)), LANE),
                         _row(jnp.repeat(ssd_D[l], 64)), _row(ssd_norm_w[l]), e64, e128_ssd)
        gdn_bias = jnp.concatenate([jnp.zeros((GDN_HEADS,), F32), gdn_dt_bias[l]])
        gdn_a = jnp.concatenate([jnp.zeros((GDN_HEADS,), F32), -jnp.exp(gdn_A_log[l].astype(F32))])
        u_d = _gdn_mixer(p_gd, gdn_conv_w[l], _row(gdn_bias, LANE), _row(gdn_a, LANE),
                         _row(gdn_norm_w[l]), e_beta, e_g)
        us = [u.reshape(tokens, WIDTH) for u in (u_a, u_b, u_c, u_d)]
        x2 = _merge(x2, nw, w_gate, us, w_branch[l].astype(BF16), w_out[l].astype(BF16),
                    _row(final_norm_w), final_norm=(l == depth - 1))
    return x2.reshape(b, s, d)
```

```python
import functools
import math

import numpy as np
import jax
import jax.numpy as jnp
from jax import lax
from jax.experimental import pallas as pl
from jax.experimental.pallas import tpu as pltpu

F32 = jnp.float32
BF16 = jnp.bfloat16

D_MODEL = 1024
WIDTH = 512
NORM_EPS = 1e-6
CONV_K = 4
RW_HEAD = 64
RW_LORA = 64
RW_GN_EPS = 64e-5
RET_HEADS = 8
RET_QK = 256
ROPE_BASE = 10000.0
SSD_HEADS = 8
SSD_STATE = 128
GDN_HEADS = 4
GDN_HEAD = 128

LANE = 128
SUBLANE = 8
CHUNK = 64
SSD_CHUNK = 128
TILE = 512
LIGHT_TILE = 512
RW_TILE = 512
ROWS = 1
PROJ_TILE = 512
PROJ_SUB = 256
VMEM_LIMIT = 56 * 1024 * 1024


def _dot(a, b):
    return jnp.dot(a.astype(BF16), b.astype(BF16), preferred_element_type=F32)


def _dot_nt(a, b):
    return lax.dot_general(a.astype(BF16), b.astype(BF16), (((1,), (1,)), ((), ())),
                           preferred_element_type=F32)


def _dot_tn(a, b):
    return lax.dot_general(a.astype(BF16), b.astype(BF16), (((0,), (0,)), ((), ())),
                           preferred_element_type=F32)


def _bf16_pieces(x):
    hi = x.astype(BF16)
    r1 = x - hi.astype(F32)
    mid = r1.astype(BF16)
    lo = (r1 - mid.astype(F32)).astype(BF16)
    return hi, mid, lo


def _select_rows(sel, x):
    n = x.shape[1]
    y = jnp.dot(sel.astype(BF16), jnp.concatenate(_bf16_pieces(x), axis=-1),
                preferred_element_type=F32)
    return y[:, 0:n] + y[:, n:2 * n] + y[:, 2 * n:3 * n]


def _select_cols(x, sel):
    m = x.shape[0]
    y = jnp.dot(jnp.concatenate(_bf16_pieces(x), axis=0), sel.astype(BF16),
                preferred_element_type=F32)
    return y[0:m] + y[m:2 * m] + y[2 * m:3 * m]


def _sigmoid(x):
    return 1.0 / (1.0 + jnp.exp(-x))


def _silu(x):
    return x * _sigmoid(x)


def _softplus(x):
    return jnp.maximum(x, 0.0) + jnp.log(1.0 + jnp.exp(-jnp.abs(x)))


def _iota(shape, dim):
    return lax.broadcasted_iota(jnp.int32, shape, dim)


def _lane_mask(width, lo, hi):
    lane = _iota((1, width), 1)
    return ((lane >= lo) & (lane < hi)).astype(F32)


def _chunk_cumsum(x, chunk):
    group = 2 * LANE
    r = _iota((group, group), 0)
    c = _iota((group, group), 1)
    sh = int(math.log2(chunk))
    tri = ((r >= c) & (jnp.right_shift(r, sh) == jnp.right_shift(c, sh))).astype(F32)
    parts = [_select_rows(tri, x[i:i + group]) for i in range(0, x.shape[0], group)]
    return parts[0] if len(parts) == 1 else jnp.concatenate(parts, axis=0)


def _seg64_sum(x):
    outs = []
    for j in range(x.shape[1] // LANE):
        xb = x[:, LANE * j:LANE * (j + 1)]
        lo = _iota(xb.shape, 1) < 64
        s_lo = jnp.sum(jnp.where(lo, xb, 0.0), axis=-1, keepdims=True)
        s_hi = jnp.sum(jnp.where(lo, 0.0, xb), axis=-1, keepdims=True)
        outs.append(jnp.where(lo, s_lo, s_hi))
    return outs[0] if len(outs) == 1 else jnp.concatenate(outs, axis=-1)


def _shifted(hist_ref, first, cur, n_shift):
    tile = cur.shape[0]

    @pl.when(first)
    def _():
        hist_ref[0:SUBLANE, :] = jnp.zeros((SUBLANE, cur.shape[1]), F32)

    hist_ref[SUBLANE:SUBLANE + tile, :] = cur
    outs = [hist_ref[SUBLANE - j:SUBLANE - j + tile, :] for j in range(1, n_shift + 1)]
    hist_ref[0:SUBLANE, :] = cur[tile - SUBLANE:tile, :]
    return outs


def _causal_conv(hist_ref, first, cur, w_ref):
    d1, d2, d3 = _shifted(hist_ref, first, cur, CONV_K - 1)
    return (cur * w_ref[3:4, :] + d1 * w_ref[2:3, :] + d2 * w_ref[1:2, :] + d3 * w_ref[0:1, :])


def _interleave(programs, lead):
    live = list(programs)
    for _ in range(lead):
        next(live[0], None)
    while live:
        for prog in list(live):
            try:
                next(prog)
            except StopIteration:
                live.remove(prog)


def _neumann_inverses(ns):
    steps = _neumann_inverse_steps(ns)
    while True:
        try:
            next(steps)
        except StopIteration as done:
            return done.value


def _neumann_inverse_steps(ns):
    size = ns[0].shape[0]
    assert size == LANE and len(ns) % 2 == 0
    eye = (_iota((size, size), 0) == _iota((size, size), 1)).astype(F32)
    zero = jnp.zeros((size, size), BF16)

    def weights(row):
        row = row.astype(BF16)
        return jnp.concatenate([jnp.concatenate([row[:, 0:size], zero], axis=-1),
                                jnp.concatenate([zero, row[:, size:2 * size]], axis=-1)], axis=0)

    ns = [jnp.concatenate([ns[i], ns[i + 1]], axis=-1) for i in range(0, len(ns), 2)]
    eye2 = jnp.concatenate([eye, eye], axis=-1)
    ps = [eye2 + n for n in ns]
    ns = [_dot(n, weights(n)) for n in ns]
    yield
    levels = int(math.log2(CHUNK)) - 1
    for _ in range(levels - 1):
        boths = [_dot(jnp.concatenate([p, n], axis=0), weights(n)) for p, n in zip(ps, ns)]
        ps = [p + both[:size] for p, both in zip(ps, boths)]
        ns = [both[size:] for both in boths]
        yield
    ps = [p + _dot(p, weights(n)) for p, n in zip(ps, ns)]
    return [half for p in ps for half in (p[:, 0:size], p[:, size:2 * size])]


def _proj_kernel(x_ref, nw_ref, w_ref, o_ref):
    for r in range(0, x_ref.shape[0], PROJ_SUB):
        x = x_ref[r:r + PROJ_SUB, :]
        h = x * lax.rsqrt(jnp.mean(x * x, axis=-1, keepdims=True) + NORM_EPS) * nw_ref[...]
        o_ref[r:r + PROJ_SUB, :] = jnp.dot(h.astype(BF16), w_ref[...], preferred_element_type=F32)


def _project(x2, norm_w, w):
    tokens, d = x2.shape
    n = w.shape[1]
    return pl.pallas_call(
        _proj_kernel,
        grid=(tokens // (2 * PROJ_TILE),),
        in_specs=[pl.BlockSpec((2 * PROJ_TILE, d), lambda i: (i, 0)),
                  pl.BlockSpec((1, d), lambda i: (0, 0)),
                  pl.BlockSpec((d, n), lambda i: (0, 0))],
        out_specs=pl.BlockSpec((2 * PROJ_TILE, n), lambda i: (i, 0)),
        out_shape=jax.ShapeDtypeStruct((tokens, n), F32),
        name="in_proj",
        compiler_params=pltpu.CompilerParams(dimension_semantics=("arbitrary",),
                                             vmem_limit_bytes=VMEM_LIMIT),
    )(x2, norm_w, w)


def _ret_kernel(p_ref, cos_ref, sin_ref, dmat_ref, qdec_ref, ktail_ref, gam_ref, nw_ref,
                o_ref, state_ref):
    first = pl.program_id(1) == 0

    @pl.when(first)
    def _():
        state_ref[...] = jnp.zeros(state_ref.shape, F32)

    def rotary(t):
        blocks = []
        for j in range(t.shape[1] // LANE):
            lanes = slice(LANE * j, LANE * (j + 1))
            tb = t[:, lanes]
            blocks.append(tb * cos_ref[:, lanes]
                          + pltpu.roll(tb, LANE - 1, axis=1) * sin_ref[0, :, lanes]
                          + pltpu.roll(tb, 1, axis=1) * sin_ref[1, :, lanes])
        return jnp.concatenate(blocks, axis=-1)

    q = rotary(p_ref[:, 0:256])
    k = rotary(p_ref[:, 256:512]) * (32.0 ** -0.5)
    v = p_ref[:, 512:1024]
    z = p_ref[:, 1024:1536]
    tile = q.shape[0]
    qmasks = [_lane_mask(LANE, 32 * h, 32 * h + 32).astype(BF16) for h in range(4)]
    vmasks = [_lane_mask(256, 64 * h, 64 * h + 64).astype(BF16) for h in range(4)]
    n_chunks = tile // CHUNK
    items = [(g, c) for g in range(2) for c in range(n_chunks)]

    def block(t, it, width):
        g, c = it
        return t[c * CHUNK:(c + 1) * CHUNK, width * g:width * (g + 1)]

    def stacked(blk, masks):
        return jnp.concatenate([blk * m for m in masks], axis=0)

    q_b, k_b, v_b = q.astype(BF16), k.astype(BF16), v.astype(BF16)
    q_st = {it: stacked(block(q_b, it, LANE), qmasks) for it in items}
    k_st = {it: stacked(block(k_b, it, LANE), qmasks) for it in items}
    v_st = {it: stacked(block(v_b, it, 256), vmasks) for it in items}
    q_dec = {it: stacked((block(q, it, LANE) * qdec_ref[it[0]]).astype(BF16), qmasks) for it in items}
    k_tl = {it: stacked((block(k, it, LANE) * ktail_ref[it[0]]).astype(BF16), qmasks) for it in items}
    s = {it: (_dot_nt(q_st[it], k_st[it]) * dmat_ref[it[0]]).astype(BF16) for it in items}
    y_in = {it: _dot(s[it], v_st[it]) for it in items}
    upd = {it: _dot_tn(k_tl[it], v_st[it]) for it in items}
    y_groups = []
    for g in range(2):
        state = state_ref[g]
        ys = []
        for c in range(n_chunks):
            y_st = y_in[g, c] + _dot(q_dec[g, c], state)
            state = gam_ref[g] * state + upd[g, c]
            ys.append(y_st[0:CHUNK] + y_st[CHUNK:2 * CHUNK]
                      + y_st[2 * CHUNK:3 * CHUNK] + y_st[3 * CHUNK:4 * CHUNK])
        state_ref[g] = state
        y_groups.append(jnp.concatenate(ys, axis=0))
    y = jnp.concatenate(y_groups, axis=-1)
    ms = _seg64_sum(y * y) * (1.0 / 64.0)
    y = y * lax.rsqrt(ms + NORM_EPS) * nw_ref[...]
    o_ref[...] = (y * _silu(z)).astype(o_ref.dtype)


def _ret_tables(seq):
    half = 16
    angle = 1.0 / (ROPE_BASE ** jnp.linspace(0.0, 1.0, half, dtype=F32))
    theta = jnp.arange(seq, dtype=F32)[:, None] * angle[None, :]
    cos = jnp.tile(jnp.repeat(jnp.cos(theta), 2, axis=1), (1, RET_HEADS))
    sin = jnp.tile(jnp.repeat(jnp.sin(theta), 2, axis=1), (1, RET_HEADS))
    even = (jnp.arange(RET_QK) % 2 == 0)[None, :]
    sin = jnp.stack([jnp.where(even, -sin, 0.0), jnp.where(even, 0.0, sin)])
    log_gamma = jnp.log(1.0 - jnp.exp2(-5.0 - jnp.arange(RET_HEADS, dtype=F32)))
    lg = log_gamma.reshape(2, 4)
    i = jnp.arange(CHUNK, dtype=F32)
    diff = i[:, None] - i[None, :]
    low = diff >= 0
    blocks = jnp.exp(jnp.where(low[None, None], lg[:, :, None, None] * diff[None, None], -jnp.inf))
    eye4 = jnp.eye(4, dtype=F32)
    dmat = jnp.einsum('ghij,hk->ghikj', blocks, eye4).reshape(2, 4 * CHUNK, 4 * CHUNK)
    lg_lane = jnp.repeat(lg, 32, axis=1)
    qdec = jnp.exp(lg_lane[:, None, :] * (i[None, :, None] + 1.0))
    ktail = jnp.exp(lg_lane[:, None, :] * (CHUNK - 1.0 - i[None, :, None]))
    gam_rows = jnp.repeat(jnp.exp(lg * CHUNK), 32, axis=1)
    gam = jnp.broadcast_to(gam_rows[:, :, None], (2, LANE, 256))
    return cos, sin, dmat, qdec, ktail, gam


def _ret_mixer(p, tables, norm_w):
    b, s, n = p.shape
    cos, sin, dmat, qdec, ktail, gam = tables
    full = lambda *shape: pl.BlockSpec(shape, lambda bi, i: (0,) * len(shape))
    return pl.pallas_call(
        _ret_kernel,
        grid=(b, s // LIGHT_TILE),
        in_specs=[pl.BlockSpec((None, LIGHT_TILE, n), lambda bi, i: (bi, i, 0)),
                  pl.BlockSpec((LIGHT_TILE, RET_QK), lambda bi, i: (i, 0)),
                  pl.BlockSpec((2, LIGHT_TILE, RET_QK), lambda bi, i: (0, i, 0)),
                  full(2, 4 * CHUNK, 4 * CHUNK), full(2, CHUNK, LANE), full(2, CHUNK, LANE),
                  full(2, LANE, 256), full(1, WIDTH)],
        out_specs=pl.BlockSpec((None, LIGHT_TILE, WIDTH), lambda bi, i: (bi, i, 0)),
        out_shape=jax.ShapeDtypeStruct((b, s, WIDTH), BF16),
        scratch_shapes=[pltpu.VMEM((2, LANE, 256), F32)],
        name="ret_mixer",
        compiler_params=pltpu.CompilerParams(dimension_semantics=("arbitrary", "arbitrary"),
                                             vmem_limit_bytes=VMEM_LIMIT),
    )(p, cos, sin, dmat, qdec, ktail, gam, norm_w)


def _ssd_kernel(p_ref, cw_ref, cb_ref, dtb_ref, a_ref, dskip_ref, nw_ref, e64_ref, e128_ref,
                o_ref, hist_ref, state_ref):
    first = pl.program_id(1) == 0

    @pl.when(first)
    def _():
        state_ref[...] = jnp.zeros(state_ref.shape, F32)

    xbc = _silu(_causal_conv(hist_ref, first, p_ref[:, 0:1024], cw_ref) + cb_ref[...])
    z = p_ref[:, 1024:1536]
    dt = _softplus(p_ref[:, 1536:1664] + dtb_ref[...])
    log_a = dt * a_ref[...]
    tile = xbc.shape[0]
    xs = xbc[:, 0:512]
    g_cum = _chunk_cumsum(log_a, SSD_CHUNK)
    g_all = _select_cols(g_cum, e128_ref[...])
    val = xs * _select_cols(dt, e64_ref[...])
    ri = _iota((SSD_CHUNK, SSD_CHUNK), 0)
    ci = _iota((SSD_CHUNK, SSD_CHUNK), 1)
    low = ri >= ci
    hmask = [_lane_mask(256, 64 * h, 64 * h + 64) for h in range(4)]
    n_chunks = tile // SSD_CHUNK
    items = [(g, c) for g in range(2) for c in range(n_chunks)]
    heads = range(4)

    def rows_of(t, c):
        return t[c * SSD_CHUNK:(c + 1) * SSD_CHUNK]

    bm = {(g, c): rows_of(xbc[:, 512 + LANE * g:512 + LANE * (g + 1)], c) for g, c in items}
    cm = {(g, c): rows_of(xbc[:, 768 + LANE * g:768 + LANE * (g + 1)], c) for g, c in items}
    vc = {(g, c): rows_of(val[:, 256 * g:256 * (g + 1)], c).astype(BF16) for g, c in items}
    gc = {(g, c, h): rows_of(g_all[:, LANE * (4 * g + h):LANE * (4 * g + h + 1)], c)
          for g, c in items for h in heads}
    g_last = {key: t[SSD_CHUNK - 1:SSD_CHUNK, :] for key, t in gc.items()}
    qk = {it: _dot_nt(cm[it], bm[it]) for it in items}
    s_cat = {it: jnp.concatenate(
        [qk[it] * jnp.exp(jnp.where(low, gc[(*it, h)] - gc[(*it, h)].T, -jnp.inf)) for h in heads],
        axis=-1).astype(BF16) for it in items}
    q_cat = {it: jnp.concatenate([cm[it] * jnp.exp(gc[(*it, h)]) for h in heads], axis=-1).astype(BF16)
             for it in items}
    v_st = {it: jnp.concatenate([vc[it] * m.astype(BF16) for m in hmask], axis=0) for it in items}
    y_in = {it: _dot(s_cat[it], v_st[it]) for it in items}
    k_tail = {key: (bm[key[0:2]] * jnp.exp(g_last[key] - t)).astype(BF16) for key, t in gc.items()}
    upd = {key: _dot_tn(k_tail[key], vc[key[0:2]]) * hmask[key[2]] for key in gc}
    y_groups = []
    for g in range(2):
        state = [state_ref[g, SSD_STATE * h:SSD_STATE * (h + 1), :] for h in heads]
        ys = []
        for c in range(n_chunks):
            ys.append(y_in[g, c] + _dot(q_cat[g, c], jnp.concatenate(state, axis=0)))
            state = [state[h] * jnp.concatenate([jnp.exp(g_last[g, c, h])] * 2, axis=-1) + upd[g, c, h]
                     for h in heads]
        for h in heads:
            state_ref[g, SSD_STATE * h:SSD_STATE * (h + 1), :] = state[h]
        y_groups.append(jnp.concatenate(ys, axis=0))
    y = jnp.concatenate(y_groups, axis=-1) + xs * dskip_ref[...]
    y = y * _silu(z)
    outs = []
    for g in range(2):
        yg = y[:, 256 * g:256 * (g + 1)]
        ms = jnp.sum(yg * yg, axis=-1, keepdims=True) * (1.0 / 256.0)
        outs.append(yg * lax.rsqrt(ms + NORM_EPS))
    o_ref[...] = (jnp.concatenate(outs, axis=-1) * nw_ref[...]).astype(o_ref.dtype)


def _ssd_mixer(p, conv_w, conv_b, dt_bias, a_vec, d_skip, norm_w, e64, e128):
    b, s, n = p.shape
    full = lambda *shape: pl.BlockSpec(shape, lambda bi, i: (0,) * len(shape))
    return pl.pallas_call(
        _ssd_kernel,
        grid=(b, s // LIGHT_TILE),
        in_specs=[pl.BlockSpec((None, LIGHT_TILE, n), lambda bi, i: (bi, i, 0)),
                  full(CONV_K, 1024), full(1, 1024), full(1, LANE), full(1, LANE),
                  full(1, WIDTH), full(1, WIDTH), full(LANE, WIDTH), full(LANE, SSD_HEADS * LANE)],
        out_specs=pl.BlockSpec((None, LIGHT_TILE, WIDTH), lambda bi, i: (bi, i, 0)),
        out_shape=jax.ShapeDtypeStruct((b, s, WIDTH), BF16),
        scratch_shapes=[pltpu.VMEM((SUBLANE + LIGHT_TILE, 1024), F32),
                        pltpu.VMEM((2, 4 * SSD_STATE, 256), F32)],
        name="ssd_mixer",
        compiler_params=pltpu.CompilerParams(dimension_semantics=("arbitrary", "arbitrary"),
                                             vmem_limit_bytes=VMEM_LIMIT),
    )(p, conv_w, conv_b, dt_bias, a_vec, d_skip, norm_w, e64, e128)


def _gdn_kernel(p_ref, cw_ref, dtb_ref, a_ref, nw_ref, eb_ref, eg_ref,
                o_ref, hist_ref, state_ref):
    first = pl.program_id(1) == 0

    @pl.when(first)
    def _():
        state_ref[...] = jnp.zeros(state_ref.shape, F32)

    qkv = _silu(_causal_conv(hist_ref, first, p_ref[:, 0:1536], cw_ref))
    z = p_ref[:, 1536:2048]
    ba = p_ref[:, 2048:2176]
    beta_blk = _sigmoid(ba)
    log_a = a_ref[...] * _softplus(ba + dtb_ref[...])
    tile = qkv.shape[0]
    g_all = _select_cols(_chunk_cumsum(log_a, CHUNK), eg_ref[...])
    beta_all = _select_cols(beta_blk, eb_ref[...])

    def l2n(t):
        return t * lax.rsqrt(jnp.sum(t * t, axis=-1, keepdims=True) + 1e-6)

    two = 2 * CHUNK
    ri = _iota((two, two), 0)
    ci = _iota((two, two), 1)
    same = jnp.right_shift(ri, 6) == jnp.right_shift(ci, 6)
    incl = same & (ri >= ci)
    strict = same & (ri > ci)
    n_chunks = tile // CHUNK
    items = [(c, pr) for c in range(n_chunks) for pr in range(2)]

    qh = [l2n(qkv[:, LANE * h:LANE * (h + 1)]) * (GDN_HEAD ** -0.5) for h in range(GDN_HEADS)]
    kh = [l2n(qkv[:, 512 + LANE * h:512 + LANE * (h + 1)]) for h in range(GDN_HEADS)]
    vh = [qkv[:, 1024 + LANE * h:1024 + LANE * (h + 1)] for h in range(GDN_HEADS)]

    def rows_of(ts, it):
        c, pr = it
        return jnp.concatenate([ts[h][c * CHUNK:(c + 1) * CHUNK] for h in (2 * pr, 2 * pr + 1)], axis=0)

    g_heads = [g_all[:, LANE * h:LANE * (h + 1)] for h in range(GDN_HEADS)]
    b_heads = [beta_all[:, LANE * h:LANE * (h + 1)] for h in range(GDN_HEADS)]
    gc = {it: rows_of(g_heads, it) for it in items}
    bt = {it: rows_of(b_heads, it) for it in items}
    q_r = {it: rows_of(qh, it) for it in items}
    k_r = {it: rows_of(kh, it) for it in items}
    v_r = {it: rows_of(vh, it) for it in items}
    kb_r = {it: k_r[it] * bt[it] for it in items}
    eg = {it: jnp.exp(gc[it]) for it in items}
    dec_i = {it: jnp.exp(jnp.where(incl, gc[it] - gc[it].T, -jnp.inf)) for it in items}
    g_last = {it: jnp.concatenate(
        [jnp.broadcast_to(gc[it][CHUNK - 1:CHUNK, :], (CHUNK, LANE)),
         jnp.broadcast_to(gc[it][two - 1:two, :], (CHUNK, LANE))], axis=0) for it in items}
    k_tail = {it: (k_r[it] * jnp.exp(g_last[it] - gc[it])).astype(BF16) for it in items}
    lam = {it: (jnp.exp(gc[it][CHUNK - 1:CHUNK, :]), jnp.exp(gc[it][two - 1:two, :])) for it in items}
    rhs = {it: jnp.concatenate([v_r[it] * bt[it], kb_r[it] * eg[it]], axis=-1).astype(BF16)
           for it in items}
    q_dec = {it: q_r[it] * eg[it] for it in items}
    sc = {it: _dot_nt(jnp.concatenate([kb_r[it], q_r[it]], axis=0), k_r[it]) for it in items}
    ms = [-jnp.where(strict, sc[it][0:two] * dec_i[it], 0.0) for it in items]
    attn = {it: (sc[it][two:2 * two] * dec_i[it]).astype(BF16) for it in items}
    t_invs = dict(zip(items, _neumann_inverses(ms)))
    uw = {it: _dot(t_invs[it], rhs[it]).astype(BF16) for it in items}
    au = {it: _dot(attn[it], uw[it]) for it in items}
    o_0 = {it: au[it][:, 0:LANE] for it in items}
    o_q = {it: (q_dec[it] - au[it][:, LANE:2 * LANE]).astype(BF16) for it in items}
    p_t, q_t = {}, {}
    for c, pr in items:
        for j in range(2):
            hs = slice(j * CHUNK, (j + 1) * CHUNK)
            pq = _dot_tn(k_tail[c, pr][hs],
                         jnp.concatenate([-uw[c, pr][hs, LANE:2 * LANE], uw[c, pr][hs, 0:LANE]], axis=-1))
            p_t[c, 2 * pr + j] = pq[:, 0:LANE].astype(BF16)
            q_t[c, 2 * pr + j] = pq[:, LANE:2 * LANE]

    states = [state_ref[h] for h in range(GDN_HEADS)]
    os = [[] for _ in range(GDN_HEADS)]
    for c in range(n_chunks):
        for h in range(GDN_HEADS):
            pr, j = divmod(h, 2)
            hs = slice(j * CHUNK, (j + 1) * CHUNK)
            os[h].append(_dot(o_q[c, pr][hs], states[h]) + o_0[c, pr][hs])
            states[h] = states[h] * lam[c, pr][j] + _dot(p_t[c, h], states[h]) + q_t[c, h]
    outs = []
    for h in range(GDN_HEADS):
        state_ref[h] = states[h]
        oh = jnp.concatenate(os[h], axis=0)
        ms_h = jnp.sum(oh * oh, axis=-1, keepdims=True) * (1.0 / GDN_HEAD)
        outs.append(oh * lax.rsqrt(ms_h + NORM_EPS) * nw_ref[...])
    o_ref[...] = (jnp.concatenate(outs, axis=-1) * _silu(z)).astype(o_ref.dtype)


def _gdn_mixer(p, conv_w, dt_bias, a_vec, norm_w, e_beta, e_g):
    b, s, n = p.shape
    full = lambda *shape: pl.BlockSpec(shape, lambda bi, i: (0,) * len(shape))
    return pl.pallas_call(
        _gdn_kernel,
        grid=(b, s // TILE),
        in_specs=[pl.BlockSpec((None, TILE, n), lambda bi, i: (bi, i, 0)),
                  full(CONV_K, 1536), full(1, LANE), full(1, LANE), full(1, GDN_HEAD),
                  full(LANE, GDN_HEADS * LANE), full(LANE, GDN_HEADS * LANE)],
        out_specs=pl.BlockSpec((None, TILE, WIDTH), lambda bi, i: (bi, i, 0)),
        out_shape=jax.ShapeDtypeStruct((b, s, WIDTH), BF16),
        scratch_shapes=[pltpu.VMEM((SUBLANE + TILE, 1536), F32),
                        pltpu.VMEM((GDN_HEADS, GDN_HEAD, GDN_HEAD), F32)],
        name="gdn_mixer",
        compiler_params=pltpu.CompilerParams(dimension_semantics=("arbitrary", "arbitrary"),
                                             vmem_limit_bytes=VMEM_LIMIT),
    )(p, conv_w, dt_bias, a_vec, norm_w, e_beta, e_g)


def _rwkv_kernel(p_ref, *refs):
    params, (o_ref, hist_ref, state_ref) = refs[:-3], refs[-3:]
    _interleave([_rwkv_row(p_ref.at[j], *params, o_ref.at[j], hist_ref.at[j], state_ref.at[j])
                 for j in range(p_ref.shape[0])], lead=RWKV_PREP_STEPS)


RWKV_PREP_STEPS = 7


def _rwkv_row(p_ref, mu_ref, wup_ref, w0_ref, aup_ref, a0_ref, kk_ref, ka_ref,
              rk_ref, lnw_ref, lnb_ref, o_ref, hist_ref, state_ref):
    first = pl.program_id(1) == 0

    @pl.when(first)
    def _():
        state_ref[...] = jnp.zeros(state_ref.shape, F32)

    cur = p_ref[:, 0:1664]
    (prev,) = _shifted(hist_ref, first, cur, 1)
    mixed = cur + (prev - cur) * mu_ref[...]
    r = mixed[:, 0:512]
    k = mixed[:, 512:1024]
    v = mixed[:, 1024:1536]
    lo = mixed[:, 1536:1664]
    z = p_ref[:, 1664:2176]
    tile = r.shape[0]
    yield
    logw = -math.exp(-0.5) * _sigmoid(w0_ref[...] + _dot(jnp.tanh(lo), wup_ref[...]))
    iclr = _sigmoid(a0_ref[...] + _dot(lo, aup_ref[...]))
    yield
    kkr = k * kk_ref[...]
    kk = kkr * lax.rsqrt(_seg64_sum(kkr * kkr) + 1e-6)
    k = k * (1.0 + (iclr - 1.0) * ka_ref[...])
    a = -kk
    b = kk * iclr
    yield
    g_inc = _chunk_cumsum(logw, CHUNK)
    g_exc = g_inc - logw
    yield
    e_inc = jnp.exp(g_inc)
    e_neg = jnp.exp(-g_inc)
    r_t = r * e_inc
    a_t = a * jnp.exp(g_exc)
    k_h = k * e_neg
    b_h = b * e_neg
    yield

    two = 2 * CHUNK
    ri = _iota((two, two), 0)
    ci = _iota((two, two), 1)
    same = jnp.right_shift(ri, 6) == jnp.right_shift(ci, 6)
    incl = (same & (ri >= ci)).astype(F32)
    strict = (same & (ri > ci)).astype(F32)
    m0 = _lane_mask(LANE, 0, 64).astype(BF16)
    m1 = _lane_mask(LANE, 64, LANE).astype(BF16)

    def stack(t):
        return jnp.concatenate([t * m0.astype(t.dtype), t * m1.astype(t.dtype)], axis=0)

    def dup(t):
        return jnp.concatenate([t, t], axis=0)

    strict_cat = strict[0:CHUNK] + strict[CHUNK:two]
    incl_cat = incl[0:CHUNK] + incl[CHUNK:two]

    n_chunks = tile // CHUNK
    n_pairs = WIDTH // LANE
    items = [(c, pr) for c in range(n_chunks) for pr in range(n_pairs)]

    def blk(t, it):
        c, pr = it
        return t[c * CHUNK:(c + 1) * CHUNK, LANE * pr:LANE * (pr + 1)]

    a_b, r_b, kh_b, bh_b, v_b = (t.astype(BF16) for t in (a_t, r_t, k_h, b_h, v))
    g_last = {it: g_inc[(it[0] + 1) * CHUNK - 1:(it[0] + 1) * CHUNK, LANE * it[1]:LANE * (it[1] + 1)]
              for it in items}
    tail = {it: jnp.exp(g_last[it] - blk(g_inc, it)) for it in items}
    e_last = {it: jnp.exp(g_last[it]) for it in items}
    a_st = {it: stack(blk(a_b, it)) for it in items}
    r_st = {it: stack(blk(r_b, it)) for it in items}
    yield
    kb_st = {it: jnp.concatenate([stack(blk(kh_b, it)), stack(blk(bh_b, it))], axis=0) for it in items}
    v_st = {it: stack(blk(v_b, it)) for it in items}
    kbt_st = {it: jnp.concatenate([stack((blk(k, it) * tail[it]).astype(BF16)),
                                   stack((blk(b, it) * tail[it]).astype(BF16))], axis=0) for it in items}
    yield
    sc = {it: _dot_nt(jnp.concatenate([blk(a_b, it), blk(r_b, it)], axis=0), kb_st[it]) for it in items}
    yield
    a_abs = [dup(sc[it][0:CHUNK, two:2 * two]) * strict for it in items]
    a_rb = {it: (dup(sc[it][CHUNK:two, two:2 * two]) * incl).astype(BF16) for it in items}
    a_akrk = {it: jnp.concatenate([sc[it][0:CHUNK, 0:two] * strict_cat, sc[it][CHUNK:two, 0:two] * incl_cat],
                                  axis=0).astype(BF16) for it in items}
    yield
    av = {it: _dot(a_akrk[it], v_st[it]) for it in items}
    yield
    t_invs = dict(zip(items, (yield from _neumann_inverse_steps(a_abs))))
    yield
    wu = {it: _dot(t_invs[it], jnp.concatenate([a_st[it], stack(av[it][0:CHUNK].astype(BF16))], axis=-1))
          .astype(BF16) for it in items}
    yield
    ry = {it: jnp.concatenate([r_st[it].astype(F32), stack(av[it][CHUNK:two])], axis=-1)
          + _dot(a_rb[it], wu[it]) for it in items}
    r_q = {it: ry[it][:, 0:LANE].astype(BF16) for it in items}
    y_0 = {it: ry[it][:, LANE:2 * LANE] for it in items}
    yield
    p_l = {it: _dot_tn(wu[it][:, 0:LANE], kbt_st[it][two:2 * two]).astype(BF16) for it in items}
    yield
    q_l = {it: _dot_tn(jnp.concatenate([v_st[it], wu[it][:, LANE:2 * LANE]], axis=0), kbt_st[it])
           for it in items}
    yield

    states = [state_ref[pr] for pr in range(n_pairs)]
    ys = [[] for _ in range(n_pairs)]
    for c in range(n_chunks):
        for pr in range(n_pairs):
            y_st = _dot_nt(r_q[c, pr], states[pr]) + y_0[c, pr]
            states[pr] = states[pr] * e_last[c, pr] + _dot(states[pr], p_l[c, pr]) + q_l[c, pr]
            ys[pr].append(y_st[0:CHUNK] + y_st[CHUNK:two])
    for pr in range(n_pairs):
        state_ref[pr] = states[pr]
    yield
    y = jnp.concatenate([jnp.concatenate(ys[pr], axis=0) for pr in range(n_pairs)], axis=-1)
    mu = _seg64_sum(y) * (1.0 / RW_HEAD)
    yc = y - mu
    yield
    var = _seg64_sum(yc * yc) * (1.0 / RW_HEAD)
    yn = yc * lax.rsqrt(var + RW_GN_EPS) * lnw_ref[...] + lnb_ref[...]
    bonus = _seg64_sum(r * k * rk_ref[...]) * v
    o_ref[...] = ((yn + bonus) * _silu(z)).astype(o_ref.dtype)


def _rwkv_mixer(p, mu, w_up, w0, a_up, a0, k_k, k_a, r_k, ln_w, ln_b):
    b, s, n = p.shape
    full = lambda *shape: pl.BlockSpec(shape, lambda bi, i: (0,) * len(shape))
    return pl.pallas_call(
        _rwkv_kernel,
        grid=(b // ROWS, s // RW_TILE),
        in_specs=[pl.BlockSpec((ROWS, RW_TILE, n), lambda bi, i: (bi, i, 0)),
                  full(1, 1664)] + [full(LANE, WIDTH), full(1, WIDTH)] * 2
                 + [full(1, WIDTH)] * 5,
        out_specs=pl.BlockSpec((ROWS, RW_TILE, WIDTH), lambda bi, i: (bi, i, 0)),
        out_shape=jax.ShapeDtypeStruct((b, s, WIDTH), BF16),
        scratch_shapes=[pltpu.VMEM((ROWS, SUBLANE + RW_TILE, 1664), F32),
                        pltpu.VMEM((ROWS, 4, LANE, LANE), F32)],
        name="rwkv_mixer",
        compiler_params=pltpu.CompilerParams(dimension_semantics=("arbitrary", "arbitrary"),
                                             vmem_limit_bytes=VMEM_LIMIT),
    )(p, mu, w_up, w0, a_up, a0, k_k, k_a, r_k, ln_w, ln_b)


def _merge_kernel(x_ref, nw_ref, wg_ref, ua_ref, ub_ref, uc_ref, ud_ref, wb_ref, wo_ref, fw_ref,
                  o_ref, *, final_norm):
    for r in range(0, x_ref.shape[0], PROJ_SUB):
        rows = slice(r, r + PROJ_SUB)
        x = x_ref[rows, :]
        h = (x * lax.rsqrt(jnp.mean(x * x, axis=-1, keepdims=True) + NORM_EPS) * nw_ref[...]).astype(BF16)
        merged = None
        for i, u_ref in enumerate((ua_ref, ub_ref, uc_ref, ud_ref)):
            gate = _sigmoid(jnp.dot(h, wg_ref[:, D_MODEL * i:D_MODEL * (i + 1)],
                                    preferred_element_type=F32))
            term = gate * jnp.dot(u_ref[rows, :], wb_ref[i], preferred_element_type=F32)
            merged = term if merged is None else merged + term
        out = x + jnp.dot(merged.astype(BF16), wo_ref[...], preferred_element_type=F32)
        if final_norm:
            out = out * lax.rsqrt(jnp.mean(out * out, axis=-1, keepdims=True) + NORM_EPS) * fw_ref[...]
        o_ref[rows, :] = out


def _merge(x2, norm_w, w_gate, us, w_branch, w_out, final_w, final_norm):
    tokens, d = x2.shape
    row = lambda n: pl.BlockSpec((PROJ_TILE, n), lambda i: (i, 0))
    const = lambda *shape: pl.BlockSpec(shape, lambda i: (0,) * len(shape))
    return pl.pallas_call(
        functools.partial(_merge_kernel, final_norm=final_norm),
        grid=(tokens // PROJ_TILE,),
        in_specs=[row(d), const(1, d), const(d, 4 * d)] + [row(WIDTH)] * 4
                 + [const(4, WIDTH, d), const(d, d), const(1, d)],
        out_specs=row(d),
        out_shape=jax.ShapeDtypeStruct((tokens, d), F32),
        name="merge",
        compiler_params=pltpu.CompilerParams(dimension_semantics=("arbitrary",),
                                             vmem_limit_bytes=VMEM_LIMIT),
    )(x2, norm_w, w_gate, *us, w_branch, w_out, final_w)


def _pad_cols(w, n):
    return jnp.pad(w, ((0, 0), (0, n - w.shape[1])))


def _split_w_in(w_in):
    widths = (WIDTH, WIDTH, WIDTH, RW_LORA, RW_LORA, WIDTH,
              RET_QK, RET_QK, WIDTH, WIDTH,
              WIDTH + 4 * SSD_STATE, WIDTH, SSD_HEADS,
              3 * WIDTH, WIDTH, GDN_HEADS, GDN_HEADS,
              4 * D_MODEL)
    offs = np.cumsum((0,) + widths)
    seg = [w_in[:, int(offs[i]):int(offs[i + 1])] for i in range(len(widths))]
    (rw_r, rw_k, rw_v, rw_wlo, rw_alo, rw_z, rt_q, rt_k, rt_v, rt_z,
     sd_xbc, sd_z, sd_dt, gd_qkv, gd_z, gd_b, gd_a, gates) = seg

    w_rw = jnp.concatenate([rw_r, rw_k, rw_v, rw_wlo, rw_alo, rw_z], axis=1)
    w_rt = jnp.concatenate([rt_q, rt_k, rt_v, rt_z], axis=1)
    w_sd = jnp.concatenate([sd_xbc, sd_z, _pad_cols(sd_dt, LANE)], axis=1)
    w_gd = jnp.concatenate([gd_qkv, gd_z, _pad_cols(jnp.concatenate([gd_b, gd_a], axis=1), LANE)],
                           axis=1)
    return [w.astype(BF16) for w in (w_rw, w_rt, w_sd, w_gd, gates)]


def _head_expand(n_heads, width, offset=0):
    e = np.zeros((LANE, n_heads * width), np.float32)
    for h in range(n_heads):
        e[offset + h, h * width:(h + 1) * width] = 1.0
    return jnp.asarray(e)


def _row(v, n=None):
    v = v.reshape(1, -1).astype(F32)
    return v if n is None else _pad_cols(v, n)


def kernel(x, norm_w, w_in, rwkv_mu_rkv, rwkv_mu_wa, rwkv_w_up, rwkv_w0, rwkv_a_up, rwkv_a0,
           rwkv_k_k, rwkv_k_a, rwkv_r_k, rwkv_ln_w, rwkv_ln_b, ret_norm_w, ssd_conv_w, ssd_conv_b,
           ssd_dt_bias, ssd_A_log, ssd_D, ssd_norm_w, gdn_conv_w, gdn_dt_bias, gdn_A_log, gdn_norm_w,
           w_branch, w_out, final_norm_w):
    b, s, d = x.shape
    depth = norm_w.shape[0]
    tokens = b * s
    ret_tables = _ret_tables(s)
    e64 = _head_expand(SSD_HEADS, 64)
    e128_ssd = _head_expand(SSD_HEADS, LANE)
    e_beta = _head_expand(GDN_HEADS, LANE, 0)
    e_g = _head_expand(GDN_HEADS, LANE, GDN_HEADS)
    zeros_lora = jnp.zeros((RW_LORA, WIDTH), F32)
    x2 = x.reshape(tokens, d)
    for l in range(depth):
        w_rw, w_rt, w_sd, w_gd, w_gate = _split_w_in(w_in[l])
        nw = _row(norm_w[l])
        p_rw = _project(x2, nw, w_rw).reshape(b, s, -1)
        p_rt = _project(x2, nw, w_rt).reshape(b, s, -1)
        p_sd = _project(x2, nw, w_sd).reshape(b, s, -1)
        p_gd = _project(x2, nw, w_gd).reshape(b, s, -1)

        mu = jnp.concatenate([rwkv_mu_rkv[l].reshape(1, -1), rwkv_mu_wa[l].reshape(1, -1)], axis=1)
        w_up = jnp.concatenate([rwkv_w_up[l], zeros_lora], axis=0).astype(BF16)
        a_up = jnp.concatenate([zeros_lora, rwkv_a_up[l]], axis=0).astype(BF16)
        u_a = _rwkv_mixer(p_rw, mu, w_up, _row(rwkv_w0[l]), a_up, _row(rwkv_a0[l]),
                          _row(rwkv_k_k[l]), _row(rwkv_k_a[l]), _row(rwkv_r_k[l]),
                          _row(rwkv_ln_w[l]), _row(rwkv_ln_b[l]))
        u_b = _ret_mixer(p_rt, ret_tables, _row(ret_norm_w[l]))
        u_c = _ssd_mixer(p_sd, ssd_conv_w[l], _row(ssd_conv_b[l]), _row(ssd_dt_bias[l], LANE),
                         _row(-jnp.exp(ssd_A_log[l].astype(F32)), LANE),
                         _row(jnp.repeat(ssd_D[l], 64)), _row(ssd_norm_w[l]), e64, e128_ssd)
        gdn_bias = jnp.concatenate([jnp.zeros((GDN_HEADS,), F32), gdn_dt_bias[l]])
        gdn_a = jnp.concatenate([jnp.zeros((GDN_HEADS,), F32), -jnp.exp(gdn_A_log[l].astype(F32))])
        u_d = _gdn_mixer(p_gd, gdn_conv_w[l], _row(gdn_bias, LANE), _row(gdn_a, LANE),
                         _row(gdn_norm_w[l]), e_beta, e_g)
        us = [u.reshape(tokens, WIDTH) for u in (u_a, u_b, u_c, u_d)]
        x2 = _merge(x2, nw, w_gate, us, w_branch[l].astype(BF16), w_out[l].astype(BF16),
                    _row(final_norm_w), final_norm=(l == depth - 1))
    return x2.reshape(b, s, d)
```
